```python
import math
import jax, jax.numpy as jnp
from jax import lax
import numpy as np

D_MODEL = 1024
BATCH = 16
SEQ = 4096
DEPTH = 2
DEC_BATCH = 32
DEC_SEQ = 16
PAST_LEN = 1024

CHUNK = 64
N_MIXERS = 2
N_A = (DEPTH + 1) // 2
N_B = DEPTH // 2
EPS = 1e-5

EXPAND = 2
D_INNER = EXPAND * D_MODEL
SSM_HEAD_DIM = 64
SSM_HEADS = D_INNER // SSM_HEAD_DIM
SSM_STATE = 128
SSM_GROUPS = 8
HEADS_PER_GROUP = SSM_HEADS // SSM_GROUPS
GN = SSM_GROUPS * SSM_STATE
CONV_WIDTH = 4
CONV_DIM = D_INNER + 2 * GN
SSD_CHUNK = CHUNK
A_PROJ = D_INNER + CONV_DIM + SSM_HEADS
NORM_GROUP = D_INNER // SSM_GROUPS

G_WIDTH = EXPAND * D_MODEL
G_CHUNK = 128
G_GROUPS = 8
G_GROUP_DIM = G_WIDTH // G_GROUPS
B_PROJ = 3 * G_WIDTH

kernel_name = 'hybrid_ssd_gmlp_streaming_step'


def rms_norm(x, w):
    xf = x.astype(jnp.float32)
    y = xf * lax.rsqrt(jnp.mean(xf * xf, axis=-1, keepdims=True) + EPS)
    return (y * w.astype(jnp.float32)).astype(x.dtype)


def layer_norm(x, w, b):
    xf = x.astype(jnp.float32)
    xc = xf - jnp.mean(xf, axis=-1, keepdims=True)
    y = xc * lax.rsqrt(jnp.mean(xc * xc, axis=-1, keepdims=True) + EPS)
    return (y * w.astype(jnp.float32) + b.astype(jnp.float32)).astype(x.dtype)


def gated_group_rms_norm(y, z, w):
    g = y.astype(jnp.float32) * jax.nn.silu(z.astype(jnp.float32))
    shp = g.shape
    g = g.reshape(shp[:-1] + (SSM_GROUPS, NORM_GROUP))
    g = g * lax.rsqrt(jnp.mean(g * g, axis=-1, keepdims=True) + EPS)
    return g.reshape(shp) * w.astype(jnp.float32)


def causal_dwconv(xbc, conv_prev, w, b):
    seq_len = xbc.shape[1]
    xp = jnp.concatenate([conv_prev.astype(xbc.dtype), xbc], axis=1)
    out = b + w[0] * xp[:, 0:seq_len]
    for k in range(1, CONV_WIDTH):
        out = out + w[k] * xp[:, k:k + seq_len]
    return out, xp[:, seq_len:]


def ssd_scan(x, dt, a, bm, cm, h0):
    f32 = jnp.float32
    bsz, seq_len = x.shape[:2]
    q = min(SSD_CHUNK, seq_len)
    assert seq_len % q == 0
    nc = seq_len // q

    def to_chunks(t):
        t = t.astype(f32).reshape((bsz, nc, q) + t.shape[2:])
        return jnp.moveaxis(t, 1, 0)

    xs = to_chunks(x.reshape(bsz, seq_len, SSM_GROUPS, HEADS_PER_GROUP, SSM_HEAD_DIM))
    dts = to_chunks(dt.reshape(bsz, seq_len, SSM_GROUPS, HEADS_PER_GROUP))
    bs = to_chunks(bm)
    cs = to_chunks(cm)
    a_g = a.astype(f32).reshape(SSM_GROUPS, HEADS_PER_GROUP)
    causal = jnp.tril(jnp.ones((q, q), dtype=bool))[None, :, :, None, None]

    def step(h, inp):
        xc, dtc, bc, cc = inp
        la = jnp.cumsum(dtc * a_g, axis=1)
        seg = la[:, :, None] - la[:, None, :]
        decay = jnp.exp(jnp.where(causal, seg, -jnp.inf))
        cb = jnp.einsum('btgn,bsgn->btsg', cc, bc)
        y_in = jnp.einsum('btsg,btsgr,bsgr,bsgrp->btgrp', cb, decay, dtc, xc)
        y_st = jnp.einsum('btgn,bgrpn->btgrp', cc, h) * jnp.exp(la)[..., None]
        last = la[:, -1]
        w_in = jnp.exp(last[:, None] - la) * dtc
        h_new = h * jnp.exp(last)[..., None, None] + jnp.einsum('bsgn,bsgr,bsgrp->bgrpn', bc, w_in, xc)
        return h_new, y_in + y_st

    h0g = h0.astype(f32).reshape(bsz, SSM_GROUPS, HEADS_PER_GROUP, SSM_HEAD_DIM, SSM_STATE)
    h_last, ys = lax.scan(step, h0g, (xs, dts, bs, cs))
    y = jnp.moveaxis(ys, 0, 1).reshape(bsz, seq_len, SSM_HEADS, SSM_HEAD_DIM)
    return y, h_last.reshape(bsz, SSM_HEADS, SSM_HEAD_DIM, SSM_STATE)


def mamba2_mixer(h, conv_prev, ssm_prev, w_in, conv_w, conv_b, dt_bias, a_log, d_skip, norm_w, w_out):
    bsz, seq_len = h.shape[:2]
    proj = h @ w_in
    z = proj[..., :D_INNER]
    xbc = proj[..., D_INNER:D_INNER + CONV_DIM]
    dt_raw = proj[..., D_INNER + CONV_DIM:]
    xbc_c, conv_new = causal_dwconv(xbc, conv_prev, conv_w, conv_b)
    xbc_c = jax.nn.silu(xbc_c)
    xs = xbc_c[..., :D_INNER].reshape(bsz, seq_len, SSM_HEADS, SSM_HEAD_DIM)
    bm = xbc_c[..., D_INNER:D_INNER + GN].reshape(bsz, seq_len, SSM_GROUPS, SSM_STATE)
    cm = xbc_c[..., D_INNER + GN:].reshape(bsz, seq_len, SSM_GROUPS, SSM_STATE)
    dt = jax.nn.softplus(dt_raw.astype(jnp.float32) + dt_bias.astype(jnp.float32))
    a = -jnp.exp(a_log.astype(jnp.float32))
    y, ssm_new = ssd_scan(xs, dt, a, bm, cm, ssm_prev)
    y = y + d_skip.astype(jnp.float32)[:, None] * xs.astype(jnp.float32)
    y = gated_group_rms_norm(y.reshape(bsz, seq_len, D_INNER), z, norm_w).astype(h.dtype)
    return y @ w_out, conv_new, ssm_new.astype(ssm_prev.dtype)


def gmlp_mixer(h, w_in, ln_w, ln_b, w_sp, b_sp, w_out):
    bsz, seq_len = h.shape[:2]
    proj = h @ w_in
    u = jax.nn.gelu(proj[..., :G_WIDTH], approximate=False)
    v = jax.nn.gelu(proj[..., G_WIDTH:2 * G_WIDTH], approximate=False)
    z = proj[..., 2 * G_WIDTH:]
    v = layer_norm(v, ln_w, ln_b)
    q = min(G_CHUNK, seq_len)
    assert seq_len % q == 0
    nk = seq_len // q
    pos = jnp.arange(q)
    mask = (pos[None, :] // CHUNK) <= (pos[:, None] // CHUNK)
    wm = jnp.where(mask[None], w_sp[:, :q, :q], 0)
    vc = v.reshape(bsz, nk, q, G_GROUPS, G_GROUP_DIM)
    s = jnp.einsum('gts,bksgc->bktgc', wm, vc) + b_sp[:, :q].T[:, :, None]
    y = (u * s.reshape(bsz, seq_len, G_WIDTH) * jax.nn.silu(z)).astype(h.dtype)
    return y @ w_out, v


def setup_inputs(seed: int = 0) -> dict:
    key = jax.random.key(seed)
    ks = jax.random.split(key, 20)
    f32 = jnp.float32

    def nrm(k, shape, scale):
        return scale * jax.random.normal(k, shape, f32)

    x_prompt = nrm(ks[0], (BATCH, SEQ, D_MODEL), 1.0)
    x_sample = nrm(ks[1], (DEC_BATCH, DEC_SEQ, D_MODEL), 1.0)
    cache_conv = nrm(ks[2], (N_A, DEC_BATCH, CONV_WIDTH - 1, CONV_DIM), 1.0)
    state_ssm = nrm(ks[3], (N_A, DEC_BATCH, SSM_HEADS, SSM_HEAD_DIM, SSM_STATE), 0.1)
    norm_w = 1.0 + nrm(ks[4], (DEPTH, D_MODEL), 0.1)
    final_norm_w = 1.0 + nrm(ks[5], (D_MODEL,), 0.1)
    a_w_in = nrm(ks[6], (N_A, D_MODEL, A_PROJ), D_MODEL ** -0.5)
    a_conv_w = nrm(ks[7], (N_A, CONV_WIDTH, CONV_DIM), CONV_WIDTH ** -0.5)
    a_conv_b = nrm(ks[8], (N_A, CONV_DIM), 0.02)
    dt0 = jnp.exp(jax.random.uniform(ks[9], (N_A, SSM_HEADS), f32, math.log(1e-3), math.log(1e-1)))
    a_dt_bias = dt0 + jnp.log(-jnp.expm1(-dt0))
    a_log = jnp.log(jax.random.uniform(ks[10], (N_A, SSM_HEADS), f32, 1.0, 16.0))
    a_d = 1.0 + nrm(ks[11], (N_A, SSM_HEADS), 0.1)
    a_norm_w = 1.0 + nrm(ks[12], (N_A, D_INNER), 0.1)
    a_w_out = nrm(ks[13], (N_A, D_INNER, D_MODEL), D_INNER ** -0.5)
    b_w_in = nrm(ks[14], (N_B, D_MODEL, B_PROJ), D_MODEL ** -0.5)
    b_ln_w = 1.0 + nrm(ks[15], (N_B, G_WIDTH), 0.1)
    b_ln_b = nrm(ks[16], (N_B, G_WIDTH), 0.02)
    b_w_sp = nrm(ks[17], (N_B, G_GROUPS, G_CHUNK, G_CHUNK), G_CHUNK ** -0.5)
    b_b_sp = 1.0 + nrm(ks[18], (N_B, G_GROUPS, G_CHUNK), 0.1)
    b_w_out = nrm(ks[19], (N_B, G_WIDTH, D_MODEL), G_WIDTH ** -0.5)
    return {'x_prompt': x_prompt, 'x_sample': x_sample,
            'cache_conv': cache_conv, 'state_ssm': state_ssm,
            'norm_w': norm_w, 'final_norm_w': final_norm_w,
            'a_w_in': a_w_in, 'a_conv_w': a_conv_w, 'a_conv_b': a_conv_b,
            'a_dt_bias': a_dt_bias, 'a_log': a_log, 'a_d': a_d,
            'a_norm_w': a_norm_w, 'a_w_out': a_w_out,
            'b_w_in': b_w_in, 'b_ln_w': b_ln_w, 'b_ln_b': b_ln_b,
            'b_w_sp': b_w_sp, 'b_b_sp': b_b_sp, 'b_w_out': b_w_out}


def reference(x_prompt, x_sample, cache_conv, state_ssm, norm_w, final_norm_w,
              a_w_in, a_conv_w, a_conv_b, a_dt_bias, a_log, a_d, a_norm_w, a_w_out,
              b_w_in, b_ln_w, b_ln_b, b_w_sp, b_b_sp, b_w_out):
    assert PAST_LEN % G_CHUNK == 0
    hp, hs = x_prompt, x_sample
    bp = x_prompt.shape[0]
    conv_p, ssm_p, conv_s, ssm_s, v_s = [], [], [], [], []
    for i in range(DEPTH):
        j = i // N_MIXERS
        hn_p = rms_norm(hp, norm_w[i])
        hn_s = rms_norm(hs, norm_w[i])
        if i % N_MIXERS == 0:
            a_par = (a_w_in[j], a_conv_w[j], a_conv_b[j], a_dt_bias[j], a_log[j],
                     a_d[j], a_norm_w[j], a_w_out[j])
            conv0 = jnp.zeros((bp, CONV_WIDTH - 1, CONV_DIM), hp.dtype)
            ssm0 = jnp.zeros((bp, SSM_HEADS, SSM_HEAD_DIM, SSM_STATE), jnp.float32)
            out_p, c_p, s_p = mamba2_mixer(hn_p, conv0, ssm0, *a_par)
            out_s, c_s, s_s = mamba2_mixer(hn_s, cache_conv[j], state_ssm[j], *a_par)
            conv_p.append(c_p)
            ssm_p.append(s_p)
            conv_s.append(c_s)
            ssm_s.append(s_s)
        else:
            b_par = (b_w_in[j], b_ln_w[j], b_ln_b[j], b_w_sp[j], b_b_sp[j], b_w_out[j])
            out_p, _ = gmlp_mixer(hn_p, *b_par)
            out_s, v_new = gmlp_mixer(hn_s, *b_par)
            v_s.append(v_new)
        hp = hp + out_p
        hs = hs + out_s
    y_prompt = rms_norm(hp, final_norm_w)
    y_sample = rms_norm(hs, final_norm_w)
    return (y_prompt, y_sample, jnp.stack(conv_p), jnp.stack(ssm_p),
            jnp.stack(conv_s), jnp.stack(ssm_s), jnp.stack(v_s))
```

```python
import functools

import jax
import jax.numpy as jnp
from jax import lax
from jax.experimental import pallas as pl
from jax.experimental.pallas import tpu as pltpu

F32 = jnp.float32
BF16 = jnp.bfloat16
HIGHEST = lax.Precision.HIGHEST

EPS = 1e-5
D_MODEL = 1024
D_INNER = 2048
SSM_HEADS = 32
SSM_HEAD_DIM = 64
SSM_STATE = 128
SSM_GROUPS = 8
HEADS_PER_GROUP = 4
GROUP_COLS = HEADS_PER_GROUP * SSM_HEAD_DIM
CONV_WIDTH = 4
CONV_DIM = 4096
NORM_GROUP = 256
G_WIDTH = 2048
G_GROUPS = 8
G_GROUP_DIM = 256
G_CHUNK = 128
CAUSAL_CHUNK = 64
ZXBC = D_INNER + CONV_DIM
LANES = 128
CARRY_ROW = 8
VMEM_LIMIT_BYTES = 56 * 1024 * 1024
NEG_BIG = -1e30

NT_DIMS = (((1,), (1,)), ((), ()))
TN_DIMS = (((0,), (0,)), ((), ()))


def _silu(x):
    return x / (1.0 + jnp.exp(-x))


def _softplus(x):
    return jnp.maximum(x, 0.0) + jnp.log1p(jnp.exp(-jnp.abs(x)))


def _gelu(x):
    return 0.5 * x * (1.0 + lax.erf(x * 0.7071067811865476))


def _rms(x, w):
    return x * lax.rsqrt(jnp.mean(x * x, axis=-1, keepdims=True) + EPS) * w


def _expand_heads(v):
    lane = lax.broadcasted_iota(jnp.int32, (v.shape[0], LANES), 1)
    first = lane < SSM_HEAD_DIM
    pieces = [jnp.where(first, v[:, 2 * j:2 * j + 1], v[:, 2 * j + 1:2 * j + 2])
              for j in range(SSM_HEADS // 2)]
    return jnp.concatenate(pieces, axis=1)


def _ssd_kernel(x_ref, rmsw_ref, win_ref, wdt_ref, wdtT_ref, convw_ref, convb_ref,
                dtb_ref, dtbT_ref, alog_ref, alogT_ref, dexp_ref, normw_ref, wout_ref,
                conv0_ref, ssm0_ref,
                out_ref, convn_ref, ssmn_ref,
                xbc_scr, z_scr, dt_scr, dta_scr, y_scr, h_scr, *, S, T, q):
    t_id = pl.program_id(1)
    last_tile = pl.num_programs(1) - 1
    M = S * T

    @pl.when(t_id == 0)
    def _load_state():
        for s in range(S):
            xbc_scr[s, CARRY_ROW - 3:CARRY_ROW, :] = conv0_ref[s]
            for g in range(SSM_GROUPS):
                hg = ssm0_ref[s, HEADS_PER_GROUP * g:HEADS_PER_GROUP * (g + 1)]
                h_scr[s, g] = hg.reshape(GROUP_COLS, SSM_STATE).T

    x = x_ref[...].reshape(M, D_MODEL)
    hn = _rms(x, rmsw_ref[...])
    hn_bf = hn.astype(BF16)
    for n in range(ZXBC // 1024):
        blk = jnp.dot(hn_bf, win_ref[:, n * 1024:(n + 1) * 1024], preferred_element_type=F32)
        if n < D_INNER // 1024:
            z_scr[:, n * 1024:(n + 1) * 1024] = blk
        else:
            c0 = n * 1024 - D_INNER
            for s in range(S):
                xbc_scr[s, CARRY_ROW:CARRY_ROW + T, c0:c0 + 1024] = blk[s * T:(s + 1) * T]

    dt = _softplus(jnp.dot(hn, wdt_ref[...], precision=HIGHEST, preferred_element_type=F32) + dtb_ref[...])
    dt_scr[...] = dt
    dta_scr[...] = dt * (-jnp.exp(alog_ref[...]))
    dtT = _softplus(lax.dot_general(wdtT_ref[...], hn, NT_DIMS, precision=HIGHEST,
                                    preferred_element_type=F32) + dtbT_ref[...])
    dtaT = dtT * (-jnp.exp(alogT_ref[...]))

    row = lax.broadcasted_iota(jnp.int32, (q, q), 0)
    col = lax.broadcasted_iota(jnp.int32, (q, q), 1)
    causal = row >= col
    tril = causal.astype(F32)
    lane_g = lax.broadcasted_iota(jnp.int32, (q, GROUP_COLS), 1)
    head_masks = [(lane_g >= SSM_HEAD_DIM * r) & (lane_g < SSM_HEAD_DIM * (r + 1))
                  for r in range(HEADS_PER_GROUP)]

    for s in range(S):
        for c in range(T // q):
            r0 = c * q
            g0 = s * T + r0
            acc = convb_ref[...]
            for k in range(CONV_WIDTH):
                lo = r0 + CARRY_ROW - (CONV_WIDTH - 1) + k
                acc = acc + convw_ref[k:k + 1, :] * xbc_scr[s, lo:lo + q, :]
            xc = _silu(acc)
            xs = xc[:, :D_INNER]
            b_bf = xc[:, D_INNER:D_INNER + 1024].astype(BF16)
            c_bf = xc[:, D_INNER + 1024:].astype(BF16)

            dt_c = dt_scr[g0:g0 + q, :]
            la = jnp.dot(tril, dta_scr[g0:g0 + q, :], precision=HIGHEST, preferred_element_type=F32)
            laT = lax.dot_general(dtaT[:, g0:g0 + q], tril, NT_DIMS, precision=HIGHEST,
                                  preferred_element_type=F32)
            dtT_c = dtT[:, g0:g0 + q]
            last = la[q - 1:q, :]
            e_la = _expand_heads(jnp.exp(la))
            w_in = _expand_heads(jnp.exp(last - la) * dt_c)
            e_last = _expand_heads(jnp.exp(last))
            xs_bf = xs.astype(BF16)
            xw_bf = (xs * w_in).astype(BF16)

            ys = []
            for g in range(SSM_GROUPS):
                gs = slice(g * GROUP_COLS, (g + 1) * GROUP_COLS)
                bg = b_bf[:, g * SSM_STATE:(g + 1) * SSM_STATE]
                cg = c_bf[:, g * SSM_STATE:(g + 1) * SSM_STATE]
                cb = lax.dot_general(cg, bg, NT_DIMS, preferred_element_type=F32)
                xg = xs_bf[:, gs]
                y_g = None
                for r in range(HEADS_PER_GROUP):
                    h = HEADS_PER_GROUP * g + r
                    seg = la[:, h:h + 1] - laT[h:h + 1, :]
                    dec = jnp.exp(jnp.where(causal, seg, NEG_BIG))
                    m_r = (cb * dec * dtT_c[h:h + 1, :]).astype(BF16)
                    x_r = jnp.where(head_masks[r], xg, jnp.zeros_like(xg))
                    part = jnp.dot(m_r, x_r, preferred_element_type=F32)
                    y_g = part if y_g is None else y_g + part
                hT = h_scr[s, g]
                y_st = jnp.dot(cg, hT.astype(BF16), preferred_element_type=F32)
                ys.append(y_g + y_st * e_la[:, gs])
                upd = lax.dot_general(bg, xw_bf[:, gs], TN_DIMS, preferred_element_type=F32)
                h_scr[s, g] = hT * e_last[:, gs] + upd

            y = jnp.concatenate(ys, axis=1) + dexp_ref[...] * xs
            gated = y * _silu(z_scr[g0:g0 + q, :])
            parts = []
            for k in range(D_INNER // NORM_GROUP):
                gk = gated[:, k * NORM_GROUP:(k + 1) * NORM_GROUP]
                parts.append(gk * lax.rsqrt(jnp.mean(gk * gk, axis=-1, keepdims=True) + EPS))
            y_scr[g0:g0 + q, :] = (jnp.concatenate(parts, axis=1) * normw_ref[...]).astype(BF16)

        tail = xbc_scr[s, CARRY_ROW + T - 3:CARRY_ROW + T, :]
        xbc_scr[s, CARRY_ROW - 3:CARRY_ROW, :] = tail

        @pl.when(t_id == last_tile)
        def _emit_conv(s=s, tail=tail):
            convn_ref[s] = tail

    out = jnp.dot(y_scr[...], wout_ref[...], preferred_element_type=F32) + x
    out_ref[...] = out.reshape(S, T, D_MODEL)

    @pl.when(t_id == last_tile)
    def _emit_state():
        for s in range(S):
            for g in range(SSM_GROUPS):
                hg = h_scr[s, g].T.reshape(HEADS_PER_GROUP, SSM_HEAD_DIM, SSM_STATE)
                ssmn_ref[s, HEADS_PER_GROUP * g:HEADS_PER_GROUP * (g + 1)] = hg


def _const_spec(shape):
    zeros = (0,) * len(shape)
    return pl.BlockSpec(shape, lambda b, t: zeros, pipeline_mode=pl.Buffered(1))


def _ssd_layer(x, conv0, ssm0, rms_w, w_zxbc, w_dt, w_dtT, conv_w, conv_b, dtb, dtbT, alog, alogT,
               d_exp, norm_w, w_out, *, S, T, q):
    nb, seq, _ = x.shape
    assert nb % S == 0 and seq % T == 0 and T % q == 0
    M = S * T
    kern = functools.partial(_ssd_kernel, S=S, T=T, q=q)
    in_specs = [
        pl.BlockSpec((S, T, D_MODEL), lambda b, t: (b, t, 0)),
        _const_spec((1, D_MODEL)),
        _const_spec((D_MODEL, ZXBC)),
        _const_spec((D_MODEL, LANES)),
        _const_spec((SSM_HEADS, D_MODEL)),
        _const_spec((CONV_WIDTH, CONV_DIM)),
        _const_spec((1, CONV_DIM)),
        _const_spec((1, LANES)),
        _const_spec((SSM_HEADS, 1)),
        _const_spec((1, LANES)),
        _const_spec((SSM_HEADS, 1)),
        _const_spec((1, D_INNER)),
        _const_spec((1, D_INNER)),
        _const_spec((D_INNER, D_MODEL)),
        pl.BlockSpec((S, CONV_WIDTH - 1, CONV_DIM), lambda b, t: (b, 0, 0)),
        pl.BlockSpec((S, SSM_HEADS, SSM_HEAD_DIM, SSM_STATE), lambda b, t: (b, 0, 0, 0)),
    ]
    out_specs = [
        pl.BlockSpec((S, T, D_MODEL), lambda b, t: (b, t, 0)),
        pl.BlockSpec((S, CONV_WIDTH - 1, CONV_DIM), lambda b, t: (b, 0, 0)),
        pl.BlockSpec((S, SSM_HEADS, SSM_HEAD_DIM, SSM_STATE), lambda b, t: (b, 0, 0, 0)),
    ]
    out_shape = [
        jax.ShapeDtypeStruct((nb, seq, D_MODEL), F32),
        jax.ShapeDtypeStruct((nb, CONV_WIDTH - 1, CONV_DIM), F32),
        jax.ShapeDtypeStruct((nb, SSM_HEADS, SSM_HEAD_DIM, SSM_STATE), F32),
    ]
    scratch = [
        pltpu.VMEM((S, CARRY_ROW + T, CONV_DIM), F32),
        pltpu.VMEM((M, D_INNER), F32),
        pltpu.VMEM((M, LANES), F32),
        pltpu.VMEM((M, LANES), F32),
        pltpu.VMEM((M, D_INNER), BF16),
        pltpu.VMEM((S, SSM_GROUPS, SSM_STATE, GROUP_COLS), F32),
    ]
    return pl.pallas_call(
        kern,
        grid=(nb // S, seq // T),
        in_specs=in_specs,
        out_specs=out_specs,
        out_shape=out_shape,
        scratch_shapes=scratch,
        compiler_params=pltpu.CompilerParams(
            dimension_semantics=("arbitrary", "arbitrary"),
            vmem_limit_bytes=VMEM_LIMIT_BYTES),
        name=f"ssd_layer_T{T}",
    )(x, rms_w, w_zxbc, w_dt, w_dtT, conv_w, conv_b, dtb, dtbT, alog, alogT, d_exp, norm_w, w_out,
      conv0, ssm0)


def _gmlp_kernel(x_ref, rmsw_ref, win_ref, lnw_ref, lnb_ref, wsp_ref, bexp_ref, wout_ref, fw_ref,
                 *refs, S, T, q, emit_v):
    if emit_v:
        out_ref, v_ref, proj_scr, y_scr = refs
    else:
        out_ref, proj_scr, y_scr = refs
        v_ref = None
    M = S * T
    x = x_ref[...].reshape(M, D_MODEL)
    hn_bf = _rms(x, rmsw_ref[...]).astype(BF16)
    for n in range(3 * G_WIDTH // 1024):
        proj_scr[:, n * 1024:(n + 1) * 1024] = jnp.dot(
            hn_bf, win_ref[:, n * 1024:(n + 1) * 1024], preferred_element_type=F32)

    row = lax.broadcasted_iota(jnp.int32, (q, q), 0)
    col = lax.broadcasted_iota(jnp.int32, (q, q), 1)
    block_causal = (col // CAUSAL_CHUNK) <= (row // CAUSAL_CHUNK)
    wms = [jnp.where(block_causal, wsp_ref[g, 0:q, 0:q], 0.0).astype(BF16) for g in range(G_GROUPS)]

    for s in range(S):
        for c in range(T // q):
            g0 = s * T + c * q
            u = _gelu(proj_scr[g0:g0 + q, 0:G_WIDTH])
            v = _gelu(proj_scr[g0:g0 + q, G_WIDTH:2 * G_WIDTH])
            vc = v - jnp.mean(v, axis=-1, keepdims=True)
            vn = vc * lax.rsqrt(jnp.mean(vc * vc, axis=-1, keepdims=True) + EPS) * lnw_ref[...] + lnb_ref[...]
            if emit_v:
                v_ref[s, c * q:(c + 1) * q, :] = vn
            vn_bf = vn.astype(BF16)
            parts = [jnp.dot(wms[g], vn_bf[:, g * G_GROUP_DIM:(g + 1) * G_GROUP_DIM],
                             preferred_element_type=F32) for g in range(G_GROUPS)]
            sp = jnp.concatenate(parts, axis=1) + bexp_ref[0:q, :]
            y = u * sp * _silu(proj_scr[g0:g0 + q, 2 * G_WIDTH:3 * G_WIDTH])
            y_scr[g0:g0 + q, :] = y.astype(BF16)

    out = jnp.dot(y_scr[...], wout_ref[...], preferred_element_type=F32) + x
    out_ref[...] = _rms(out, fw_ref[...]).reshape(S, T, D_MODEL)


def _gmlp_layer(x, rms_w, w_in, ln_w, ln_b, w_sp, b_exp, w_out, final_w, *, S, T, q, emit_v):
    nb, seq, _ = x.shape
    assert nb % S == 0 and seq % T == 0 and T % q == 0
    M = S * T
    kern = functools.partial(_gmlp_kernel, S=S, T=T, q=q, emit_v=emit_v)
    in_specs = [
        pl.BlockSpec((S, T, D_MODEL), lambda b, t: (b, t, 0)),
        _const_spec((1, D_MODEL)),
        _const_spec((D_MODEL, 3 * G_WIDTH)),
        _const_spec((1, G_WIDTH)),
        _const_spec((1, G_WIDTH)),
        _const_spec((G_GROUPS, G_CHUNK, G_CHUNK)),
        _const_spec((G_CHUNK, G_WIDTH)),
        _const_spec((G_WIDTH, D_MODEL)),
        _const_spec((1, D_MODEL)),
    ]
    out_specs = [pl.BlockSpec((S, T, D_MODEL), lambda b, t: (b, t, 0))]
    out_shape = [jax.ShapeDtypeStruct((nb, seq, D_MODEL), F32)]
    if emit_v:
        out_specs.append(pl.BlockSpec((S, T, G_WIDTH), lambda b, t: (b, t, 0)))
        out_shape.append(jax.ShapeDtypeStruct((nb, seq, G_WIDTH), F32))
    scratch = [pltpu.VMEM((M, 3 * G_WIDTH), F32), pltpu.VMEM((M, G_WIDTH), BF16)]
    return pl.pallas_call(
        kern,
        grid=(nb // S, seq // T),
        in_specs=in_specs,
        out_specs=out_specs,
        out_shape=out_shape,
        scratch_shapes=scratch,
        compiler_params=pltpu.CompilerParams(
            dimension_semantics=("arbitrary", "arbitrary"),
            vmem_limit_bytes=VMEM_LIMIT_BYTES),
        name=f"gmlp_layer_T{T}",
    )(x, rms_w, w_in, ln_w, ln_b, w_sp, b_exp, w_out, final_w)


def _tiling(seq, chunk):
    q = min(chunk, seq)
    T = min(seq, 256)
    return T, q


def kernel(x_prompt, x_sample, cache_conv, state_ssm, norm_w, final_norm_w, a_w_in, a_conv_w, a_conv_b,
           a_dt_bias, a_log, a_d, a_norm_w, a_w_out, b_w_in, b_ln_w, b_ln_b, b_w_sp, b_b_sp, b_w_out):
    assert a_w_in.shape[0] == 1 and b_w_in.shape[0] == 1 and norm_w.shape[0] == 2

    w_in = a_w_in[0]
    w_zxbc = w_in[:, :ZXBC].astype(BF16)
    w_dt_cols = w_in[:, ZXBC:]
    w_dt = jnp.pad(w_dt_cols, ((0, 0), (0, LANES - SSM_HEADS)))
    w_dtT = w_dt_cols.T
    pad_heads = lambda v: jnp.pad(v.reshape(1, SSM_HEADS), ((0, 0), (0, LANES - SSM_HEADS)))
    ssd_params = dict(
        rms_w=norm_w[0].reshape(1, D_MODEL), w_zxbc=w_zxbc, w_dt=w_dt, w_dtT=w_dtT,
        conv_w=a_conv_w[0], conv_b=a_conv_b[0].reshape(1, CONV_DIM),
        dtb=pad_heads(a_dt_bias[0]), dtbT=a_dt_bias[0].reshape(SSM_HEADS, 1),
        alog=pad_heads(a_log[0]), alogT=a_log[0].reshape(SSM_HEADS, 1),
        d_exp=jnp.repeat(a_d[0], SSM_HEAD_DIM).reshape(1, D_INNER),
        norm_w=a_norm_w[0].reshape(1, D_INNER), w_out=a_w_out[0].astype(BF16))
    gmlp_params = dict(
        rms_w=norm_w[1].reshape(1, D_MODEL), w_in=b_w_in[0].astype(BF16),
        ln_w=b_ln_w[0].reshape(1, G_WIDTH), ln_b=b_ln_b[0].reshape(1, G_WIDTH),
        w_sp=b_w_sp[0], b_exp=jnp.repeat(b_b_sp[0].T, G_GROUP_DIM, axis=1),
        w_out=b_w_out[0].astype(BF16), final_w=final_norm_w.reshape(1, D_MODEL))

    bp, lp, _ = x_prompt.shape
    bs, ls, _ = x_sample.shape

    Tp, qp = _tiling(lp, CAUSAL_CHUNK)
    conv0 = jnp.zeros((bp, CONV_WIDTH - 1, CONV_DIM), F32)
    ssm0 = jnp.zeros((bp, SSM_HEADS, SSM_HEAD_DIM, SSM_STATE), F32)
    hp, conv_p, ssm_p = _ssd_layer(x_prompt, conv0, ssm0, **ssd_params, S=1, T=Tp, q=qp)
    Tg, qg = _tiling(lp, G_CHUNK)
    (y_prompt,) = _gmlp_layer(hp, **gmlp_params, S=1, T=Tg, q=qg, emit_v=False)

    Ts, qs = _tiling(ls, CAUSAL_CHUNK)
    hs, conv_s, ssm_s = _ssd_layer(x_sample, cache_conv[0], state_ssm[0], **ssd_params, S=4, T=Ts, q=qs)
    Tg, qg = _tiling(ls, G_CHUNK)
    y_sample, v_s = _gmlp_layer(hs, **gmlp_params, S=4, T=Tg, q=qg, emit_v=True)

    return (y_prompt, y_sample, conv_p[None], ssm_p[None], conv_s[None], ssm_s[None], v_s[None])
```

```python
import functools

import numpy as np
import jax
import jax.numpy as jnp
from jax import lax
from jax.experimental import pallas as pl
from jax.experimental.pallas import tpu as pltpu

F32 = jnp.float32
BF16 = jnp.bfloat16

EPS = 1e-5
D_MODEL = 1024
D_INNER = 2048
SSM_HEADS = 32
SSM_HEAD_DIM = 64
SSM_STATE = 128
SSM_GROUPS = 8
HEADS_PER_GROUP = 4
GROUP_COLS = HEADS_PER_GROUP * SSM_HEAD_DIM
CONV_WIDTH = 4
CONV_DIM = 4096
NORM_GROUP = 256
G_WIDTH = 2048
G_GROUPS = 8
G_GROUP_DIM = 256
G_CHUNK = 128
CAUSAL_CHUNK = 64
ZXBC = D_INNER + CONV_DIM
LANES = 128
SUBLANES = 8
VMEM_LIMIT_BYTES = 56 * 1024 * 1024
NEG_BIG = -1e30

NT_DIMS = (((1,), (1,)), ((), ()))
TN_DIMS = (((0,), (0,)), ((), ()))


def _silu(x):
    return x / (1.0 + jnp.exp(-x))


def _softplus(x):
    return jnp.maximum(x, 0.0) + jnp.log1p(jnp.exp(-jnp.abs(x)))


def _gelu(x):
    return 0.5 * x * (1.0 + lax.erf(x * 0.7071067811865476))


def _rms(x, w):
    return x * lax.rsqrt(jnp.mean(x * x, axis=-1, keepdims=True) + EPS) * w


def _split3(v):
    hi = v.astype(BF16)
    r1 = v - hi.astype(F32)
    mid = r1.astype(BF16)
    lo = (r1 - mid.astype(F32)).astype(BF16)
    return hi, mid, lo


def _expand_heads(v):
    lane = lax.broadcasted_iota(jnp.int32, (v.shape[0], LANES), 1)
    first = lane < SSM_HEAD_DIM
    pieces = [jnp.where(first, v[:, 2 * j:2 * j + 1], v[:, 2 * j + 1:2 * j + 2])
              for j in range(SSM_HEADS // 2)]
    return jnp.concatenate(pieces, axis=1)


def _chunk_token(row, q):
    r = row % q
    return r // SUBLANES + (q // SUBLANES) * (r % SUBLANES)


def _token_row(tau, q):
    nv = q // SUBLANES
    return SUBLANES * (tau % nv) + tau // nv


def _ssd_kernel(x_ref, rmsw_ref, win_ref, wdt_ref, perm_ref, permT_ref, cums_ref, convw_ref, convb_ref,
                dtb_ref, alog_ref, dexp_ref, normw_ref, wout_ref, conv0_ref, ssm0_ref,
                out_ref, convn_ref, ssmn_ref,
                xbc_scr, z_scr, y_scr, h_scr, *, S, T, q):
    t_id = pl.program_id(1)
    last_tile = pl.num_programs(1) - 1
    M = S * T
    nv = q // SUBLANES
    n_chunks = T // q
    tail_rows = [_token_row(q - (CONV_WIDTH - 1) + m, q) for m in range(CONV_WIDTH - 1)]

    @pl.when(t_id == 0)
    def _load_state():
        for s in range(S):
            xbc_scr[s, 0:q, :] = jnp.zeros((q, CONV_DIM), F32)
            for m, rho in enumerate(tail_rows):
                xbc_scr[s, rho:rho + 1, :] = conv0_ref[s, m:m + 1, :]
            for g in range(SSM_GROUPS):
                hg = ssm0_ref[s, HEADS_PER_GROUP * g:HEADS_PER_GROUP * (g + 1)]
                h_scr[s, g] = hg.reshape(GROUP_COLS, SSM_STATE).T

    x = x_ref[...].reshape(M, D_MODEL)
    hn = _rms(x, rmsw_ref[...])
    hn_bf = hn.astype(BF16)
    hn_p = jnp.dot(perm_ref[...], hn_bf, preferred_element_type=F32).astype(BF16)
    for n in range(ZXBC // 1024):
        blk = jnp.dot(hn_p, win_ref[:, n * 1024:(n + 1) * 1024], preferred_element_type=F32)
        if n < D_INNER // 1024:
            z_scr[:, n * 1024:(n + 1) * 1024] = blk
        else:
            c0 = n * 1024 - D_INNER
            for s in range(S):
                xbc_scr[s, q:q + T, c0:c0 + 1024] = blk[s * T:(s + 1) * T]

    hn_lo = (hn - hn_bf.astype(F32)).astype(BF16)
    p2 = jnp.dot(hn_bf, wdt_ref[...], preferred_element_type=F32)
    dt_raw = p2[:, :LANES] + p2[:, LANES:] + jnp.dot(hn_lo, wdt_ref[:, :LANES], preferred_element_type=F32)
    dt_nat = _softplus(dt_raw + dtb_ref[...])
    dta_nat = dt_nat * (-jnp.exp(alog_ref[...]))
    la3 = jnp.dot(cums_ref[...], jnp.concatenate(_split3(dta_nat), axis=1), preferred_element_type=F32)
    la = la3[:, :LANES] + (la3[:, LANES:2 * LANES] + la3[:, 2 * LANES:])
    dt3 = jnp.dot(perm_ref[...], jnp.concatenate(_split3(dt_nat), axis=1), preferred_element_type=F32)
    dt = dt3[:, :LANES] + (dt3[:, LANES:2 * LANES] + dt3[:, 2 * LANES:])
    laT = la.T
    dtT = dt.T

    tok_t = _chunk_token(lax.broadcasted_iota(jnp.int32, (q, HEADS_PER_GROUP * q), 0), q)
    tok_s = _chunk_token(lax.broadcasted_iota(jnp.int32, (q, HEADS_PER_GROUP * q), 1), q)
    causal = tok_t >= tok_s
    lane_cat = lax.broadcasted_iota(jnp.int32, (q, HEADS_PER_GROUP * q), 1)
    lane_g = lax.broadcasted_iota(jnp.int32, (q, GROUP_COLS), 1)
    head_masks = [(lane_g >= SSM_HEAD_DIM * r) & (lane_g < SSM_HEAD_DIM * (r + 1))
                  for r in range(HEADS_PER_GROUP)]
    sub_iota = lax.broadcasted_iota(jnp.int32, (SUBLANES, CONV_DIM), 0)

    for s in range(S):
        for c in range(n_chunks):
            base = q + c * q
            g0 = s * T + c * q

            def halo(v, sh, base=base, s=s):
                cur = xbc_scr[s, base + SUBLANES * v:base + SUBLANES * (v + 1), :]
                prv = xbc_scr[s, base - q + SUBLANES * v:base - q + SUBLANES * (v + 1), :]
                return jnp.where(sub_iota >= sh, pltpu.roll(cur, sh, 0), pltpu.roll(prv, sh, 0))

            acc = convb_ref[...] + convw_ref[CONV_WIDTH - 1:CONV_WIDTH, :] * xbc_scr[s, base:base + q, :]
            for d in range(1, CONV_WIDTH):
                pieces = []
                for j in range(min(d, nv)):
                    jj = j - d
                    pieces.append(halo(jj % nv, -(jj // nv)))
                if q - SUBLANES * d > 0:
                    pieces.append(xbc_scr[s, base:base + q - SUBLANES * d, :])
                src = pieces[0] if len(pieces) == 1 else jnp.concatenate(pieces, axis=0)
                acc = acc + convw_ref[CONV_WIDTH - 1 - d:CONV_WIDTH - d, :] * src
            xc = _silu(acc)
            xs = xc[:, :D_INNER]
            b_bf = xc[:, D_INNER:D_INNER + 1024].astype(BF16)
            c_bf = xc[:, D_INNER + 1024:].astype(BF16)

            la_c = la[g0:g0 + q, :]
            dt_c = dt[g0:g0 + q, :]
            laT_c = laT[:, g0:g0 + q]
            dtT_c = dtT[:, g0:g0 + q]
            last = la_c[q - 1:q, :]
            e_la = _expand_heads(jnp.exp(la_c))
            w_in = _expand_heads(jnp.exp(last - la_c) * dt_c)
            e_last = _expand_heads(jnp.exp(last))
            xs_bf = xs.astype(BF16)
            xw_bf = (xs * w_in).astype(BF16)
            if q == SSM_HEAD_DIM:
                la_cols = _expand_heads(la_c)

            ys = []
            for g in range(SSM_GROUPS):
                gs = slice(g * GROUP_COLS, (g + 1) * GROUP_COLS)
                h0 = HEADS_PER_GROUP * g
                bg = b_bf[:, g * SSM_STATE:(g + 1) * SSM_STATE]
                cg = c_bf[:, g * SSM_STATE:(g + 1) * SSM_STATE]
                cb = lax.dot_general(cg, jnp.concatenate([bg] * HEADS_PER_GROUP, axis=0), NT_DIMS,
                                     preferred_element_type=F32)
                if q == SSM_HEAD_DIM:
                    la_col = la_cols[:, gs]
                else:
                    la_col = la_c[:, h0 + HEADS_PER_GROUP - 1:h0 + HEADS_PER_GROUP]
                    for r in range(HEADS_PER_GROUP - 2, -1, -1):
                        la_col = jnp.where(lane_cat < (r + 1) * q, la_c[:, h0 + r:h0 + r + 1], la_col)
                la_row = jnp.concatenate([laT_c[h0 + r:h0 + r + 1, :] for r in range(HEADS_PER_GROUP)], axis=1)
                dt_row = jnp.concatenate([dtT_c[h0 + r:h0 + r + 1, :] for r in range(HEADS_PER_GROUP)], axis=1)
                dec = jnp.exp(jnp.where(causal, la_col - la_row, NEG_BIG))
                m_cat = (cb * dec * dt_row).astype(BF16)
                xg = xs_bf[:, gs]
                x_bd = jnp.concatenate([jnp.where(head_masks[r], xg, jnp.zeros_like(xg))
                                        for r in range(HEADS_PER_GROUP)], axis=0)
                y_in = jnp.dot(m_cat, x_bd, preferred_element_type=F32)
                hT = h_scr[s, g]
                y_st = jnp.dot(cg, hT.astype(BF16), preferred_element_type=F32)
                ys.append(y_in + y_st * e_la[:, gs])
                upd = lax.dot_general(bg, xw_bf[:, gs], TN_DIMS, preferred_element_type=F32)
                h_scr[s, g] = hT * e_last[:, gs] + upd

            y = jnp.concatenate(ys, axis=1) + dexp_ref[...] * xs
            gated = y * _silu(z_scr[g0:g0 + q, :])
            parts = []
            for k in range(D_INNER // NORM_GROUP):
                gk = gated[:, k * NORM_GROUP:(k + 1) * NORM_GROUP]
                parts.append(gk * lax.rsqrt(jnp.mean(gk * gk, axis=-1, keepdims=True) + EPS))
            y_scr[g0:g0 + q, :] = (jnp.concatenate(parts, axis=1) * normw_ref[...]).astype(BF16)

        @pl.when(t_id == last_tile)
        def _emit_conv(s=s):
            for m, rho in enumerate(tail_rows):
                convn_ref[s, m:m + 1, :] = xbc_scr[s, T + rho:T + rho + 1, :]

        xbc_scr[s, 0:q, :] = xbc_scr[s, T:T + q, :]

    y_nat = jnp.dot(permT_ref[...], y_scr[...], preferred_element_type=F32).astype(BF16)
    out = jnp.dot(y_nat, wout_ref[...], preferred_element_type=F32) + x
    out_ref[...] = out.reshape(S, T, D_MODEL)

    @pl.when(t_id == last_tile)
    def _emit_state():
        for s in range(S):
            for g in range(SSM_GROUPS):
                hg = h_scr[s, g].T.reshape(HEADS_PER_GROUP, SSM_HEAD_DIM, SSM_STATE)
                ssmn_ref[s, HEADS_PER_GROUP * g:HEADS_PER_GROUP * (g + 1)] = hg


def _const_spec(shape):
    zeros = (0,) * len(shape)
    return pl.BlockSpec(shape, lambda b, t: zeros, pipeline_mode=pl.Buffered(1))


def _row_order_matrices(M, q):
    rows = np.arange(M)
    chunk = rows // q
    tok_of_row = _chunk_token(rows, q)
    same = chunk[:, None] == chunk[None, :]
    perm = same & (tok_of_row[:, None] == (rows % q)[None, :])
    cums = same & (tok_of_row[:, None] >= (rows % q)[None, :])
    as_bf = lambda m: jnp.asarray(m, dtype=BF16)
    return as_bf(perm), as_bf(perm.T), as_bf(cums)


def _ssd_layer(x, conv0, ssm0, rms_w, w_zxbc, w_dt, conv_w, conv_b, dtb, alog, d_exp, norm_w, w_out, *, S, T, q):
    nb, seq, _ = x.shape
    assert nb % S == 0 and seq % T == 0 and T % q == 0 and q % SUBLANES == 0
    M = S * T
    kern = functools.partial(_ssd_kernel, S=S, T=T, q=q)
    in_specs = [
        pl.BlockSpec((S, T, D_MODEL), lambda b, t: (b, t, 0)),
        _const_spec((1, D_MODEL)),
        _const_spec((D_MODEL, ZXBC)),
        _const_spec((D_MODEL, 2 * LANES)),
        _const_spec((M, M)),
        _const_spec((M, M)),
        _const_spec((M, M)),
        _const_spec((CONV_WIDTH, CONV_DIM)),
        _const_spec((1, CONV_DIM)),
        _const_spec((1, LANES)),
        _const_spec((1, LANES)),
        _const_spec((1, D_INNER)),
        _const_spec((1, D_INNER)),
        _const_spec((D_INNER, D_MODEL)),
        pl.BlockSpec((S, CONV_WIDTH - 1, CONV_DIM), lambda b, t: (b, 0, 0)),
        pl.BlockSpec((S, SSM_HEADS, SSM_HEAD_DIM, SSM_STATE), lambda b, t: (b, 0, 0, 0)),
    ]
    out_specs = [
        pl.BlockSpec((S, T, D_MODEL), lambda b, t: (b, t, 0)),
        pl.BlockSpec((S, CONV_WIDTH - 1, CONV_DIM), lambda b, t: (b, 0, 0)),
        pl.BlockSpec((S, SSM_HEADS, SSM_HEAD_DIM, SSM_STATE), lambda b, t: (b, 0, 0, 0)),
    ]
    out_shape = [
        jax.ShapeDtypeStruct((nb, seq, D_MODEL), F32),
        jax.ShapeDtypeStruct((nb, CONV_WIDTH - 1, CONV_DIM), F32),
        jax.ShapeDtypeStruct((nb, SSM_HEADS, SSM_HEAD_DIM, SSM_STATE), F32),
    ]
    scratch = [
        pltpu.VMEM((S, q + T, CONV_DIM), F32),
        pltpu.VMEM((M, D_INNER), F32),
        pltpu.VMEM((M, D_INNER), BF16),
        pltpu.VMEM((S, SSM_GROUPS, SSM_STATE, GROUP_COLS), F32),
    ]
    return pl.pallas_call(
        kern,
        grid=(nb // S, seq // T),
        in_specs=in_specs,
        out_specs=out_specs,
        out_shape=out_shape,
        scratch_shapes=scratch,
        compiler_params=pltpu.CompilerParams(
            dimension_semantics=("arbitrary", "arbitrary"),
            vmem_limit_bytes=VMEM_LIMIT_BYTES),
        name=f"ssd_layer_T{T}",
    )(x, rms_w, w_zxbc, w_dt, *_row_order_matrices(M, q), conv_w, conv_b, dtb, alog, d_exp, norm_w, w_out,
      conv0, ssm0)


def _gmlp_kernel(x_ref, rmsw_ref, win_ref, lnw_ref, lnb_ref, wsp_ref, bexp_ref, wout_ref, fw_ref,
                 *refs, S, T, q, emit_v):
    if emit_v:
        out_ref, v_ref, proj_scr, y_scr = refs
    else:
        out_ref, proj_scr, y_scr = refs
        v_ref = None
    M = S * T
    x = x_ref[...].reshape(M, D_MODEL)
    hn_bf = _rms(x, rmsw_ref[...]).astype(BF16)
    for n in range(3 * G_WIDTH // 1024):
        proj_scr[:, n * 1024:(n + 1) * 1024] = jnp.dot(
            hn_bf, win_ref[:, n * 1024:(n + 1) * 1024], preferred_element_type=F32)

    row = lax.broadcasted_iota(jnp.int32, (q, q), 0)
    col = lax.broadcasted_iota(jnp.int32, (q, q), 1)
    block_causal = (col // CAUSAL_CHUNK) <= (row // CAUSAL_CHUNK)
    wms = [jnp.where(block_causal, wsp_ref[g, 0:q, 0:q], 0.0).astype(BF16) for g in range(G_GROUPS)]

    for s in range(S):
        for c in range(T // q):
            g0 = s * T + c * q
            u = _gelu(proj_scr[g0:g0 + q, 0:G_WIDTH])
            v = _gelu(proj_scr[g0:g0 + q, G_WIDTH:2 * G_WIDTH])
            vc = v - jnp.mean(v, axis=-1, keepdims=True)
            vn = vc * lax.rsqrt(jnp.mean(vc * vc, axis=-1, keepdims=True) + EPS) * lnw_ref[...] + lnb_ref[...]
            if emit_v:
                v_ref[s, c * q:(c + 1) * q, :] = vn
            vn_bf = vn.astype(BF16)
            parts = [jnp.dot(wms[g], vn_bf[:, g * G_GROUP_DIM:(g + 1) * G_GROUP_DIM],
                             preferred_element_type=F32) for g in range(G_GROUPS)]
            sp = jnp.concatenate(parts, axis=1) + bexp_ref[0:q, :]
            y = u * sp * _silu(proj_scr[g0:g0 + q, 2 * G_WIDTH:3 * G_WIDTH])
            y_scr[g0:g0 + q, :] = y.astype(BF16)

    out = jnp.dot(y_scr[...], wout_ref[...], preferred_element_type=F32) + x
    out_ref[...] = _rms(out, fw_ref[...]).reshape(S, T, D_MODEL)


def _gmlp_layer(x, rms_w, w_in, ln_w, ln_b, w_sp, b_exp, w_out, final_w, *, S, T, q, emit_v):
    nb, seq, _ = x.shape
    assert nb % S == 0 and seq % T == 0 and T % q == 0
    M = S * T
    kern = functools.partial(_gmlp_kernel, S=S, T=T, q=q, emit_v=emit_v)
    in_specs = [
        pl.BlockSpec((S, T, D_MODEL), lambda b, t: (b, t, 0)),
        _const_spec((1, D_MODEL)),
        _const_spec((D_MODEL, 3 * G_WIDTH)),
        _const_spec((1, G_WIDTH)),
        _const_spec((1, G_WIDTH)),
        _const_spec((G_GROUPS, G_CHUNK, G_CHUNK)),
        _const_spec((G_CHUNK, G_WIDTH)),
        _const_spec((G_WIDTH, D_MODEL)),
        _const_spec((1, D_MODEL)),
    ]
    out_specs = [pl.BlockSpec((S, T, D_MODEL), lambda b, t: (b, t, 0))]
    out_shape = [jax.ShapeDtypeStruct((nb, seq, D_MODEL), F32)]
    if emit_v:
        out_specs.append(pl.BlockSpec((S, T, G_WIDTH), lambda b, t: (b, t, 0)))
        out_shape.append(jax.ShapeDtypeStruct((nb, seq, G_WIDTH), F32))
    scratch = [pltpu.VMEM((M, 3 * G_WIDTH), F32), pltpu.VMEM((M, G_WIDTH), BF16)]
    return pl.pallas_call(
        kern,
        grid=(nb // S, seq // T),
        in_specs=in_specs,
        out_specs=out_specs,
        out_shape=out_shape,
        scratch_shapes=scratch,
        compiler_params=pltpu.CompilerParams(
            dimension_semantics=("arbitrary", "arbitrary"),
            vmem_limit_bytes=VMEM_LIMIT_BYTES),
        name=f"gmlp_layer_T{T}",
    )(x, rms_w, w_in, ln_w, ln_b, w_sp, b_exp, w_out, final_w)


def _tiling(seq, chunk):
    q = min(chunk, seq)
    T = min(seq, 256)
    return T, q


def kernel(x_prompt, x_sample, cache_conv, state_ssm, norm_w, final_norm_w, a_w_in, a_conv_w, a_conv_b,
           a_dt_bias, a_log, a_d, a_norm_w, a_w_out, b_w_in, b_ln_w, b_ln_b, b_w_sp, b_b_sp, b_w_out):
    assert a_w_in.shape[0] == 1 and b_w_in.shape[0] == 1 and norm_w.shape[0] == 2

    w_in = a_w_in[0]
    w_dt = jnp.pad(w_in[:, ZXBC:], ((0, 0), (0, LANES - SSM_HEADS)))
    w_dt_hi = w_dt.astype(BF16)
    w_dt_lo = (w_dt - w_dt_hi.astype(F32)).astype(BF16)
    pad_heads = lambda v: jnp.pad(v.reshape(1, SSM_HEADS), ((0, 0), (0, LANES - SSM_HEADS)))
    ssd_params = dict(
        rms_w=norm_w[0].reshape(1, D_MODEL), w_zxbc=w_in[:, :ZXBC].astype(BF16),
        w_dt=jnp.concatenate([w_dt_hi, w_dt_lo], axis=1),
        conv_w=a_conv_w[0], conv_b=a_conv_b[0].reshape(1, CONV_DIM),
        dtb=pad_heads(a_dt_bias[0]), alog=pad_heads(a_log[0]),
        d_exp=jnp.repeat(a_d[0], SSM_HEAD_DIM).reshape(1, D_INNER),
        norm_w=a_norm_w[0].reshape(1, D_INNER), w_out=a_w_out[0].astype(BF16))
    gmlp_params = dict(
        rms_w=norm_w[1].reshape(1, D_MODEL), w_in=b_w_in[0].astype(BF16),
        ln_w=b_ln_w[0].reshape(1, G_WIDTH), ln_b=b_ln_b[0].reshape(1, G_WIDTH),
        w_sp=b_w_sp[0], b_exp=jnp.repeat(b_b_sp[0].T, G_GROUP_DIM, axis=1),
        w_out=b_w_out[0].astype(BF16), final_w=final_norm_w.reshape(1, D_MODEL))

    bp, lp, _ = x_prompt.shape
    bs, ls, _ = x_sample.shape

    Tp, qp = _tiling(lp, CAUSAL_CHUNK)
    conv0 = jnp.zeros((bp, CONV_WIDTH - 1, CONV_DIM), F32)
    ssm0 = jnp.zeros((bp, SSM_HEADS, SSM_HEAD_DIM, SSM_STATE), F32)
    hp, conv_p, ssm_p = _ssd_layer(x_prompt, conv0, ssm0, **ssd_params, S=1, T=Tp, q=qp)
    Tg, qg = _tiling(lp, G_CHUNK)
    (y_prompt,) = _gmlp_layer(hp, **gmlp_params, S=1, T=Tg, q=qg, emit_v=False)

    Ts, qs = _tiling(ls, CAUSAL_CHUNK)
    hs, conv_s, ssm_s = _ssd_layer(x_sample, cache_conv[0], state_ssm[0], **ssd_params, S=4, T=Ts, q=qs)
    Tg, qg = _tiling(ls, G_CHUNK)
    y_sample, v_s = _gmlp_layer(hs, **gmlp_params, S=4, T=Tg, q=qg, emit_v=True)

    return (y_prompt, y_sample, conv_p[None], ssm_p[None], conv_s[None], ssm_s[None], v_s[None])
```

```python
import functools

import numpy as np
import jax
import jax.numpy as jnp
from jax import lax
from jax.experimental import pallas as pl
from jax.experimental.pallas import tpu as pltpu

F32 = jnp.float32
BF16 = jnp.bfloat16

EPS = 1e-5
D_MODEL = 1024
D_INNER = 2048
SSM_HEADS = 32
SSM_HEAD_DIM = 64
SSM_STATE = 128
SSM_GROUPS = 8
HEADS_PER_GROUP = 4
GROUP_COLS = HEADS_PER_GROUP * SSM_HEAD_DIM
CONV_WIDTH = 4
CONV_DIM = 4096
NORM_GROUP = 256
G_WIDTH = 2048
G_GROUPS = 8
G_GROUP_DIM = 256
G_CHUNK = 128
CAUSAL_CHUNK = 64
ZXBC = D_INNER + CONV_DIM
PROJ_BLOCK = 1024
LANES = 128
SUBLANES = 8
VMEM_LIMIT_BYTES = 58 * 1024 * 1024
NEG_BIG = -1e30

NT_DIMS = (((1,), (1,)), ((), ()))
TN_DIMS = (((0,), (0,)), ((), ()))


def _silu(x):
    return x / (1.0 + jnp.exp(-x))


def _softplus(x):
    return jnp.maximum(x, 0.0) + jnp.log1p(jnp.exp(-jnp.abs(x)))


def _gelu(x):
    return 0.5 * x * (1.0 + lax.erf(x * 0.7071067811865476))


def _rms(x, w):
    return x * lax.rsqrt(jnp.mean(x * x, axis=-1, keepdims=True) + EPS) * w


def _split3(v):
    hi = v.astype(BF16)
    r1 = v - hi.astype(F32)
    mid = r1.astype(BF16)
    lo = (r1 - mid.astype(F32)).astype(BF16)
    return hi, mid, lo


def _sum3(p):
    return p[:, :LANES] + (p[:, LANES:2 * LANES] + p[:, 2 * LANES:])


def _expand_heads(v):
    lane = lax.broadcasted_iota(jnp.int32, (v.shape[0], LANES), 1)
    first = lane < SSM_HEAD_DIM
    pieces = [jnp.where(first, v[:, 2 * j:2 * j + 1], v[:, 2 * j + 1:2 * j + 2])
              for j in range(SSM_HEADS // 2)]
    return jnp.concatenate(pieces, axis=1)


def _chunk_token(row, q):
    r = row % q
    return r // SUBLANES + (q // SUBLANES) * (r % SUBLANES)


def _token_row(tau, q):
    nv = q // SUBLANES
    return SUBLANES * (tau % nv) + tau // nv


def _ssd_kernel(x_ref, rmsw_ref, win_ref, wdt_ref, perm_ref, permT_ref, cums_ref, convw_ref, convb_ref,
                dtb_ref, alog_ref, dexp_ref, normw_ref, wout_ref, conv0_ref, ssm0_ref,
                out_ref, convn_ref, ssmn_ref,
                xbc_scr, before_scr, z_scr, xres_scr, la_scr, dt_scr, laT_scr, dtT_scr, y_scr, h_scr,
                *, S, T, q, nt):
    i = pl.program_id(0)
    M = S * T
    nv = q // SUBLANES
    n_chunks = T // q
    tail_rows = [_token_row(q - (CONV_WIDTH - 1) + m, q) for m in range(CONV_WIDTH - 1)]
    mix_tile = jnp.maximum(i - 1, 0) % nt
    first_of_stream = (i >= 1) & (mix_tile == 0)
    last_of_stream = (i >= 1) & (mix_tile == nt - 1)

    @pl.when(i == 0)
    def _prime():
        xbc_scr[1] = jnp.zeros((S, T, CONV_DIM), F32)
        z_scr[1] = jnp.zeros((M, D_INNER), F32)
        xres_scr[1] = jnp.zeros((M, D_MODEL), F32)
        la_scr[1] = jnp.zeros((M, LANES), F32)
        dt_scr[1] = jnp.zeros((M, LANES), F32)
        laT_scr[1] = jnp.zeros((LANES, M), F32)
        dtT_scr[1] = jnp.zeros((LANES, M), F32)
        before_scr[...] = jnp.zeros((S, q, CONV_DIM), F32)
        h_scr[...] = jnp.zeros((S, SSM_GROUPS, SSM_STATE, GROUP_COLS), F32)

    @pl.when(first_of_stream)
    def _load_state():
        for s in range(S):
            before_scr[s] = jnp.zeros((q, CONV_DIM), F32)
            for m, rho in enumerate(tail_rows):
                before_scr[s, rho:rho + 1, :] = conv0_ref[s, m:m + 1, :]
            for g in range(SSM_GROUPS):
                hg = ssm0_ref[s, HEADS_PER_GROUP * g:HEADS_PER_GROUP * (g + 1)]
                h_scr[s, g] = hg.reshape(GROUP_COLS, SSM_STATE).T

    def step(w, r):
        x = x_ref[...].reshape(M, D_MODEL)
        xres_scr[w] = x
        hn = _rms(x, rmsw_ref[...])
        hn_bf = hn.astype(BF16)
        hn_p = jnp.dot(perm_ref[...], hn_bf, preferred_element_type=F32).astype(BF16)

        def proj_block(n):
            blk = jnp.dot(hn_p, win_ref[:, n * PROJ_BLOCK:(n + 1) * PROJ_BLOCK], preferred_element_type=F32)
            c0 = n * PROJ_BLOCK
            if c0 < D_INNER:
                z_scr[w, :, c0:c0 + PROJ_BLOCK] = blk
            else:
                for s in range(S):
                    xbc_scr[w, s, :, c0 - D_INNER:c0 - D_INNER + PROJ_BLOCK] = blk[s * T:(s + 1) * T]

        def dt_block():
            hn_lo = (hn - hn_bf.astype(F32)).astype(BF16)
            p2 = jnp.dot(hn_bf, wdt_ref[...], preferred_element_type=F32)
            dt_raw = p2[:, :LANES] + p2[:, LANES:] + jnp.dot(hn_lo, wdt_ref[:, :LANES],
                                                             preferred_element_type=F32)
            dt_nat = _softplus(dt_raw + dtb_ref[...])
            dta_nat = dt_nat * (-jnp.exp(alog_ref[...]))
            la = _sum3(jnp.dot(cums_ref[...], jnp.concatenate(_split3(dta_nat), axis=1),
                               preferred_element_type=F32))
            dt = _sum3(jnp.dot(perm_ref[...], jnp.concatenate(_split3(dt_nat), axis=1),
                               preferred_element_type=F32))
            la_scr[w] = la
            dt_scr[w] = dt
            laT_scr[w] = la.T
            dtT_scr[w] = dt.T

        tok_t = _chunk_token(lax.broadcasted_iota(jnp.int32, (q, HEADS_PER_GROUP * q), 0), q)
        tok_s = _chunk_token(lax.broadcasted_iota(jnp.int32, (q, HEADS_PER_GROUP * q), 1), q)
        causal = tok_t >= tok_s
        lane_cat = lax.broadcasted_iota(jnp.int32, (q, HEADS_PER_GROUP * q), 1)
        lane_g = lax.broadcasted_iota(jnp.int32, (q, GROUP_COLS), 1)
        head_masks = [(lane_g >= SSM_HEAD_DIM * hr) & (lane_g < SSM_HEAD_DIM * (hr + 1))
                      for hr in range(HEADS_PER_GROUP)]
        sub_iota = lax.broadcasted_iota(jnp.int32, (SUBLANES, CONV_DIM), 0)

        def mix_chunk(s, c):
            base = c * q
            g0 = s * T + c * q

            def halo(v, sh):
                cur = xbc_scr[r, s, base + SUBLANES * v:base + SUBLANES * (v + 1), :]
                if c == 0:
                    prv = before_scr[s, SUBLANES * v:SUBLANES * (v + 1), :]
                else:
                    prv = xbc_scr[r, s, base - q + SUBLANES * v:base - q + SUBLANES * (v + 1), :]
                return jnp.where(sub_iota >= sh, pltpu.roll(cur, sh, 0), pltpu.roll(prv, sh, 0))

            acc = convb_ref[...] + convw_ref[CONV_WIDTH - 1:CONV_WIDTH, :] * xbc_scr[r, s, base:base + q, :]
            for d in range(1, CONV_WIDTH):
                pieces = []
                for j in range(min(d, nv)):
                    jj = j - d
                    pieces.append(halo(jj % nv, -(jj // nv)))
                if q - SUBLANES * d > 0:
                    pieces.append(xbc_scr[r, s, base:base + q - SUBLANES * d, :])
                src = pieces[0] if len(pieces) == 1 else jnp.concatenate(pieces, axis=0)
                acc = acc + convw_ref[CONV_WIDTH - 1 - d:CONV_WIDTH - d, :] * src
            xc = _silu(acc)
            xs = xc[:, :D_INNER]
            b_bf = xc[:, D_INNER:D_INNER + 1024].astype(BF16)
            c_bf = xc[:, D_INNER + 1024:].astype(BF16)

            la_c = la_scr[r, g0:g0 + q, :]
            dt_c = dt_scr[r, g0:g0 + q, :]
            laT_c = laT_scr[r, :, g0:g0 + q]
            dtT_c = dtT_scr[r, :, g0:g0 + q]
            last = la_c[q - 1:q, :]
            e_la = _expand_heads(jnp.exp(la_c))
            w_in = _expand_heads(jnp.exp(last - la_c) * dt_c)
            e_last = _expand_heads(jnp.exp(last))
            xs_bf = xs.astype(BF16)
            xw_bf = (xs * w_in).astype(BF16)
            if q == SSM_HEAD_DIM:
                la_cols = _expand_heads(la_c)

            ys = []
            for g in range(SSM_GROUPS):
                gs = slice(g * GROUP_COLS, (g + 1) * GROUP_COLS)
                h0 = HEADS_PER_GROUP * g
                bg = b_bf[:, g * SSM_STATE:(g + 1) * SSM_STATE]
                cg = c_bf[:, g * SSM_STATE:(g + 1) * SSM_STATE]
                cb = lax.dot_general(cg, jnp.concatenate([bg] * HEADS_PER_GROUP, axis=0), NT_DIMS,
                                     preferred_element_type=F32)
                if q == SSM_HEAD_DIM:
                    la_col = la_cols[:, gs]
                else:
                    la_col = la_c[:, h0 + HEADS_PER_GROUP - 1:h0 + HEADS_PER_GROUP]
                    for hr in range(HEADS_PER_GROUP - 2, -1, -1):
                        la_col = jnp.where(lane_cat < (hr + 1) * q, la_c[:, h0 + hr:h0 + hr + 1], la_col)
                la_row = jnp.concatenate([laT_c[h0 + hr:h0 + hr + 1, :] for hr in range(HEADS_PER_GROUP)], axis=1)
                dt_row = jnp.concatenate([dtT_c[h0 + hr:h0 + hr + 1, :] for hr in range(HEADS_PER_GROUP)], axis=1)
                dec = jnp.exp(jnp.where(causal, la_col - la_row, NEG_BIG))
                m_cat = (cb * dec * dt_row).astype(BF16)
                xg = xs_bf[:, gs]
                x_bd = jnp.concatenate([jnp.where(head_masks[hr], xg, jnp.zeros_like(xg))
                                        for hr in range(HEADS_PER_GROUP)], axis=0)
                y_in = jnp.dot(m_cat, x_bd, preferred_element_type=F32)
                hT = h_scr[s, g]
                y_st = jnp.dot(cg, hT.astype(BF16), preferred_element_type=F32)
                ys.append(y_in + y_st * e_la[:, gs])
                upd = lax.dot_general(bg, xw_bf[:, gs], TN_DIMS, preferred_element_type=F32)
                h_scr[s, g] = hT * e_last[:, gs] + upd

            y = jnp.concatenate(ys, axis=1) + dexp_ref[...] * xs
            gated = y * _silu(z_scr[r, g0:g0 + q, :])
            parts = []
            for k in range(D_INNER // NORM_GROUP):
                gk = gated[:, k * NORM_GROUP:(k + 1) * NORM_GROUP]
                parts.append(gk * lax.rsqrt(jnp.mean(gk * gk, axis=-1, keepdims=True) + EPS))
            y_scr[g0:g0 + q, :] = (jnp.concatenate(parts, axis=1) * normw_ref[...]).astype(BF16)

        def mix_finish():
            for s in range(S):
                before_scr[s] = xbc_scr[r, s, T - q:T, :]
            y_nat = jnp.dot(permT_ref[...], y_scr[...], preferred_element_type=F32).astype(BF16)
            out = jnp.dot(y_nat, wout_ref[...], preferred_element_type=F32) + xres_scr[r]
            out_ref[...] = out.reshape(S, T, D_MODEL)

            @pl.when(last_of_stream)
            def _emit_state():
                for s in range(S):
                    for m, rho in enumerate(tail_rows):
                        convn_ref[s, m:m + 1, :] = before_scr[s, rho:rho + 1, :]
                    for g in range(SSM_GROUPS):
                        hg = h_scr[s, g].T.reshape(HEADS_PER_GROUP, SSM_HEAD_DIM, SSM_STATE)
                        ssmn_ref[s, HEADS_PER_GROUP * g:HEADS_PER_GROUP * (g + 1)] = hg

        proj_items = [functools.partial(proj_block, n) for n in range(ZXBC // PROJ_BLOCK)] + [dt_block]
        mix_items = [functools.partial(mix_chunk, s, c) for s in range(S) for c in range(n_chunks)]
        proj_items[0]()
        rest = np.array_split(np.arange(1, len(proj_items)), len(mix_items))
        for mix, idxs in zip(mix_items, rest):
            mix()
            for k in idxs:
                proj_items[k]()
        mix_finish()

    @pl.when(i % 2 == 0)
    def _even():
        step(0, 1)

    @pl.when(i % 2 == 1)
    def _odd():
        step(1, 0)


def _const_spec(shape):
    zeros = (0,) * len(shape)
    return pl.BlockSpec(shape, lambda *_: zeros, pipeline_mode=pl.Buffered(1))


def _row_order_matrices(M, q):
    rows = np.arange(M)
    chunk = rows // q
    tok_of_row = _chunk_token(rows, q)
    same = chunk[:, None] == chunk[None, :]
    perm = same & (tok_of_row[:, None] == (rows % q)[None, :])
    cums = same & (tok_of_row[:, None] >= (rows % q)[None, :])
    as_bf = lambda m: jnp.asarray(m, dtype=BF16)
    return as_bf(perm), as_bf(perm.T), as_bf(cums)


def _ssd_layer(x, conv0, ssm0, rms_w, w_zxbc, w_dt, conv_w, conv_b, dtb, alog, d_exp, norm_w, w_out, *, S, T, q):
    nb, seq, _ = x.shape
    assert nb % S == 0 and seq % T == 0 and T % q == 0 and q % SUBLANES == 0
    M = S * T
    nt = seq // T
    n_tiles = (nb // S) * nt
    kern = functools.partial(_ssd_kernel, S=S, T=T, q=q, nt=nt)

    def proj_tile(i):
        ia = jnp.minimum(i, n_tiles - 1)
        return ia // nt, ia % nt

    def mix_tile(i):
        ib = jnp.maximum(i - 1, 0)
        return ib // nt, ib % nt

    in_specs = [
        pl.BlockSpec((S, T, D_MODEL), lambda i: (*proj_tile(i), 0)),
        _const_spec((1, D_MODEL)),
        _const_spec((D_MODEL, ZXBC)),
        _const_spec((D_MODEL, 2 * LANES)),
        _const_spec((M, M)),
        _const_spec((M, M)),
        _const_spec((M, M)),
        _const_spec((CONV_WIDTH, CONV_DIM)),
        _const_spec((1, CONV_DIM)),
        _const_spec((1, LANES)),
        _const_spec((1, LANES)),
        _const_spec((1, D_INNER)),
        _const_spec((1, D_INNER)),
        _const_spec((D_INNER, D_MODEL)),
        pl.BlockSpec((S, CONV_WIDTH - 1, CONV_DIM), lambda i: (mix_tile(i)[0], 0, 0)),
        pl.BlockSpec((S, SSM_HEADS, SSM_HEAD_DIM, SSM_STATE), lambda i: (mix_tile(i)[0], 0, 0, 0)),
    ]
    out_specs = [
        pl.BlockSpec((S, T, D_MODEL), lambda i: (*mix_tile(i), 0)),
        pl.BlockSpec((S, CONV_WIDTH - 1, CONV_DIM), lambda i: (mix_tile(i)[0], 0, 0)),
        pl.BlockSpec((S, SSM_HEADS, SSM_HEAD_DIM, SSM_STATE), lambda i: (mix_tile(i)[0], 0, 0, 0)),
    ]
    out_shape = [
        jax.ShapeDtypeStruct((nb, seq, D_MODEL), F32),
        jax.ShapeDtypeStruct((nb, CONV_WIDTH - 1, CONV_DIM), F32),
        jax.ShapeDtypeStruct((nb, SSM_HEADS, SSM_HEAD_DIM, SSM_STATE), F32),
    ]
    scratch = [
        pltpu.VMEM((2, S, T, CONV_DIM), F32),
        pltpu.VMEM((S, q, CONV_DIM), F32),
        pltpu.VMEM((2, M, D_INNER), F32),
        pltpu.VMEM((2, M, D_MODEL), F32),
        pltpu.VMEM((2, M, LANES), F32),
        pltpu.VMEM((2, M, LANES), F32),
        pltpu.VMEM((2, LANES, M), F32),
        pltpu.VMEM((2, LANES, M), F32),
        pltpu.VMEM((M, D_INNER), BF16),
        pltpu.VMEM((S, SSM_GROUPS, SSM_STATE, GROUP_COLS), F32),
    ]
    return pl.pallas_call(
        kern,
        grid=(n_tiles + 1,),
        in_specs=in_specs,
        out_specs=out_specs,
        out_shape=out_shape,
        scratch_shapes=scratch,
        compiler_params=pltpu.CompilerParams(
            dimension_semantics=("arbitrary",),
            vmem_limit_bytes=VMEM_LIMIT_BYTES),
        name=f"ssd_layer_T{T}",
    )(x, rms_w, w_zxbc, w_dt, *_row_order_matrices(M, q), conv_w, conv_b, dtb, alog, d_exp, norm_w, w_out,
      conv0, ssm0)


def _gmlp_kernel(x_ref, rmsw_ref, win_ref, lnw_ref, lnb_ref, wsp_ref, bexp_ref, wout_ref, fw_ref,
                 *refs, S, T, q, emit_v):
    if emit_v:
        out_ref, v_ref, proj_scr, y_scr = refs
    else:
        out_ref, proj_scr, y_scr = refs
        v_ref = None
    M = S * T
    x = x_ref[...].reshape(M, D_MODEL)
    hn_bf = _rms(x, rmsw_ref[...]).astype(BF16)
    for n in range(3 * G_WIDTH // 1024):
        proj_scr[:, n * 1024:(n + 1) * 1024] = jnp.dot(
            hn_bf, win_ref[:, n * 1024:(n + 1) * 1024], preferred_element_type=F32)

    row = lax.broadcasted_iota(jnp.int32, (q, q), 0)
    col = lax.broadcasted_iota(jnp.int32, (q, q), 1)
    block_causal = (col // CAUSAL_CHUNK) <= (row // CAUSAL_CHUNK)
    wms = [jnp.where(block_causal, wsp_ref[g, 0:q, 0:q], 0.0).astype(BF16) for g in range(G_GROUPS)]

    for s in range(S):
        for c in range(T // q):
            g0 = s * T + c * q
            u = _gelu(proj_scr[g0:g0 + q, 0:G_WIDTH])
            v = _gelu(proj_scr[g0:g0 + q, G_WIDTH:2 * G_WIDTH])
            vc = v - jnp.mean(v, axis=-1, keepdims=True)
            vn = vc * lax.rsqrt(jnp.mean(vc * vc, axis=-1, keepdims=True) + EPS) * lnw_ref[...] + lnb_ref[...]
            if emit_v:
                v_ref[s, c * q:(c + 1) * q, :] = vn
            vn_bf = vn.astype(BF16)
            parts = [jnp.dot(wms[g], vn_bf[:, g * G_GROUP_DIM:(g + 1) * G_GROUP_DIM],
                             preferred_element_type=F32) for g in range(G_GROUPS)]
            sp = jnp.concatenate(parts, axis=1) + bexp_ref[0:q, :]
            y = u * sp * _silu(proj_scr[g0:g0 + q, 2 * G_WIDTH:3 * G_WIDTH])
            y_scr[g0:g0 + q, :] = y.astype(BF16)

    out = jnp.dot(y_scr[...], wout_ref[...], preferred_element_type=F32) + x
    out_ref[...] = _rms(out, fw_ref[...]).reshape(S, T, D_MODEL)


def _gmlp_layer(x, rms_w, w_in, ln_w, ln_b, w_sp, b_exp, w_out, final_w, *, S, T, q, emit_v):
    nb, seq, _ = x.shape
    assert nb % S == 0 and seq % T == 0 and T % q == 0
    M = S * T
    kern = functools.partial(_gmlp_kernel, S=S, T=T, q=q, emit_v=emit_v)
    in_specs = [
        pl.BlockSpec((S, T, D_MODEL), lambda b, t: (b, t, 0)),
        _const_spec((1, D_MODEL)),
        _const_spec((D_MODEL, 3 * G_WIDTH)),
        _const_spec((1, G_WIDTH)),
        _const_spec((1, G_WIDTH)),
        _const_spec((G_GROUPS, G_CHUNK, G_CHUNK)),
        _const_spec((G_CHUNK, G_WIDTH)),
        _const_spec((G_WIDTH, D_MODEL)),
        _const_spec((1, D_MODEL)),
    ]
    out_specs = [pl.BlockSpec((S, T, D_MODEL), lambda b, t: (b, t, 0))]
    out_shape = [jax.ShapeDtypeStruct((nb, seq, D_MODEL), F32)]
    if emit_v:
        out_specs.append(pl.BlockSpec((S, T, G_WIDTH), lambda b, t: (b, t, 0)))
        out_shape.append(jax.ShapeDtypeStruct((nb, seq, G_WIDTH), F32))
    scratch = [pltpu.VMEM((M, 3 * G_WIDTH), F32), pltpu.VMEM((M, G_WIDTH), BF16)]
    return pl.pallas_call(
        kern,
        grid=(nb // S, seq // T),
        in_specs=in_specs,
        out_specs=out_specs,
        out_shape=out_shape,
        scratch_shapes=scratch,
        compiler_params=pltpu.CompilerParams(
            dimension_semantics=("arbitrary", "arbitrary"),
            vmem_limit_bytes=VMEM_LIMIT_BYTES),
        name=f"gmlp_layer_T{T}",
    )(x, rms_w, w_in, ln_w, ln_b, w_sp, b_exp, w_out, final_w)


def _tiling(seq, chunk):
    q = min(chunk, seq)
    T = min(seq, 256)
    return T, q


def kernel(x_prompt, x_sample, cache_conv, state_ssm, norm_w, final_norm_w, a_w_in, a_conv_w, a_conv_b,
           a_dt_bias, a_log, a_d, a_norm_w, a_w_out, b_w_in, b_ln_w, b_ln_b, b_w_sp, b_b_sp, b_w_out):
    assert a_w_in.shape[0] == 1 and b_w_in.shape[0] == 1 and norm_w.shape[0] == 2

    w_in = a_w_in[0]
    w_dt = jnp.pad(w_in[:, ZXBC:], ((0, 0), (0, LANES - SSM_HEADS)))
    w_dt_hi = w_dt.astype(BF16)
    w_dt_lo = (w_dt - w_dt_hi.astype(F32)).astype(BF16)
    pad_heads = lambda v: jnp.pad(v.reshape(1, SSM_HEADS), ((0, 0), (0, LANES - SSM_HEADS)))
    ssd_params = dict(
        rms_w=norm_w[0].reshape(1, D_MODEL), w_zxbc=w_in[:, :ZXBC].astype(BF16),
        w_dt=jnp.concatenate([w_dt_hi, w_dt_lo], axis=1),
        conv_w=a_conv_w[0], conv_b=a_conv_b[0].reshape(1, CONV_DIM),
        dtb=pad_heads(a_dt_bias[0]), alog=pad_heads(a_log[0]),
        d_exp=jnp.repeat(a_d[0], SSM_HEAD_DIM).reshape(1, D_INNER),
        norm_w=a_norm_w[0].reshape(1, D_INNER), w_out=a_w_out[0].astype(BF16))
    gmlp_params = dict(
        rms_w=norm_w[1].reshape(1, D_MODEL), w_in=b_w_in[0].astype(BF16),
        ln_w=b_ln_w[0].reshape(1, G_WIDTH), ln_b=b_ln_b[0].reshape(1, G_WIDTH),
        w_sp=b_w_sp[0], b_exp=jnp.repeat(b_b_sp[0].T, G_GROUP_DIM, axis=1),
        w_out=b_w_out[0].astype(BF16), final_w=final_norm_w.reshape(1, D_MODEL))

    bp, lp, _ = x_prompt.shape
    bs, ls, _ = x_sample.shape

    Tp, qp = _tiling(lp, CAUSAL_CHUNK)
    conv0 = jnp.zeros((bp, CONV_WIDTH - 1, CONV_DIM), F32)
    ssm0 = jnp.zeros((bp, SSM_HEADS, SSM_HEAD_DIM, SSM_STATE), F32)
    hp, conv_p, ssm_p = _ssd_layer(x_prompt, conv0, ssm0, **ssd_params, S=1, T=Tp, q=qp)
    Tg, qg = _tiling(lp, G_CHUNK)
    (y_prompt,) = _gmlp_layer(hp, **gmlp_params, S=1, T=Tg, q=qg, emit_v=False)

    Ts, qs = _tiling(ls, CAUSAL_CHUNK)
    hs, conv_s, ssm_s = _ssd_layer(x_sample, cache_conv[0], state_ssm[0], **ssd_params, S=4, T=Ts, q=qs)
    Tg, qg = _tiling(ls, G_CHUNK)
    y_sample, v_s = _gmlp_layer(hs, **gmlp_params, S=4, T=Tg, q=qg, emit_v=True)

    return (y_prompt, y_sample, conv_p[None], ssm_p[None], conv_s[None], ssm_s[None], v_s[None])
```

```python
import functools
import math

import numpy as np
import jax
import jax.numpy as jnp
from jax import lax
from jax.experimental import pallas as pl
from jax.experimental.pallas import tpu as pltpu

F32 = jnp.float32
BF16 = jnp.bfloat16

EPS = 1e-5
D_MODEL = 1024
D_INNER = 2048
SSM_HEADS = 32
SSM_HEAD_DIM = 64
SSM_STATE = 128
SSM_GROUPS = 8
HEADS_PER_GROUP = 4
GROUP_COLS = HEADS_PER_GROUP * SSM_HEAD_DIM
CONV_WIDTH = 4
CONV_DIM = 4096
NORM_GROUP = 256
G_WIDTH = 2048
G_GROUPS = 8
G_GROUP_DIM = 256
G_CHUNK = 128
CAUSAL_CHUNK = 64
ZXBC = D_INNER + CONV_DIM
PROJ_BLOCK = 1024
SSD_TILE = 256
GMLP_TILE = 512
SSD_SAMPLE_STREAMS = 4
GMLP_SAMPLE_STREAMS = 32
LANES = 128
SUBLANES = 8
VMEM_LIMIT_BYTES = 58 * 1024 * 1024
NEG_BIG = -1e30

NT_DIMS = (((1,), (1,)), ((), ()))
TN_DIMS = (((0,), (0,)), ((), ()))


def _silu(x):
    return x / (1.0 + jnp.exp(-x))


def _softplus(x):
    return jnp.maximum(x, 0.0) + jnp.log1p(jnp.exp(-jnp.abs(x)))


def _gelu(x):
    return 0.5 * x * (1.0 + lax.erf(x * 0.7071067811865476))


def _rms(x, w):
    return x * lax.rsqrt(jnp.mean(x * x, axis=-1, keepdims=True) + EPS) * w


def _split3(v):
    hi = v.astype(BF16)
    r1 = v - hi.astype(F32)
    mid = r1.astype(BF16)
    lo = (r1 - mid.astype(F32)).astype(BF16)
    return hi, mid, lo


def _sum3(p):
    return p[:, :LANES] + (p[:, LANES:2 * LANES] + p[:, 2 * LANES:])


def _expand_heads(v):
    lane = lax.broadcasted_iota(jnp.int32, (v.shape[0], LANES), 1)
    first = lane < SSM_HEAD_DIM
    pieces = [jnp.where(first, v[:, 2 * j:2 * j + 1], v[:, 2 * j + 1:2 * j + 2])
              for j in range(SSM_HEADS // 2)]
    return jnp.concatenate(pieces, axis=1)


def _chunk_token(row, q):
    r = row % q
    return r // SUBLANES + (q // SUBLANES) * (r % SUBLANES)


def _token_row(tau, q):
    nv = q // SUBLANES
    return SUBLANES * (tau % nv) + tau // nv


def _ssd_kernel(x_ref, rmsw_ref, win_ref, wdt_ref, perm_ref, permT_ref, cums_ref, convw_ref, convb_ref,
                dtb_ref, alog_ref, dexp_ref, normw_ref, wout_ref, conv0_ref, ssm0_ref,
                out_ref, convn_ref, ssmn_ref,
                xbc_scr, before_scr, z_scr, xres_scr, la_scr, dt_scr, laT_scr, dtT_scr, y_scr, h_scr,
                *, S, T, q, nt):
    i = pl.program_id(0)
    M = S * T
    nv = q // SUBLANES
    n_chunks = T // q
    tail_rows = [_token_row(q - (CONV_WIDTH - 1) + m, q) for m in range(CONV_WIDTH - 1)]
    mix_tile = jnp.maximum(i - 1, 0) % nt
    first_of_stream = (i >= 1) & (mix_tile == 0)
    last_of_stream = (i >= 1) & (mix_tile == nt - 1)

    @pl.when(i == 0)
    def _prime():
        xbc_scr[1] = jnp.zeros((S, T, CONV_DIM), F32)
        z_scr[1] = jnp.zeros((M, D_INNER), F32)
        xres_scr[1] = jnp.zeros((M, D_MODEL), F32)
        la_scr[1] = jnp.zeros((M, LANES), F32)
        dt_scr[1] = jnp.zeros((M, LANES), F32)
        laT_scr[1] = jnp.zeros((LANES, M), F32)
        dtT_scr[1] = jnp.zeros((LANES, M), F32)
        before_scr[...] = jnp.zeros((S, q, CONV_DIM), F32)
        h_scr[...] = jnp.zeros((S, SSM_GROUPS, SSM_STATE, GROUP_COLS), F32)

    @pl.when(first_of_stream)
    def _load_state():
        for s in range(S):
            before_scr[s] = jnp.zeros((q, CONV_DIM), F32)
            for m, rho in enumerate(tail_rows):
                before_scr[s, rho:rho + 1, :] = conv0_ref[s, m:m + 1, :]
            for g in range(SSM_GROUPS):
                hg = ssm0_ref[s, HEADS_PER_GROUP * g:HEADS_PER_GROUP * (g + 1)]
                h_scr[s, g] = hg.reshape(GROUP_COLS, SSM_STATE).T

    def step(w, r):
        x = x_ref[...].reshape(M, D_MODEL)
        xres_scr[w] = x
        hn = _rms(x, rmsw_ref[...])
        hn_bf = hn.astype(BF16)
        hn_p = jnp.dot(perm_ref[...], hn_bf, preferred_element_type=F32).astype(BF16)

        def proj_block(n):
            blk = jnp.dot(hn_p, win_ref[:, n * PROJ_BLOCK:(n + 1) * PROJ_BLOCK], preferred_element_type=F32)
            c0 = n * PROJ_BLOCK
            if c0 < D_INNER:
                z_scr[w, :, c0:c0 + PROJ_BLOCK] = blk
            else:
                for s in range(S):
                    xbc_scr[w, s, :, c0 - D_INNER:c0 - D_INNER + PROJ_BLOCK] = blk[s * T:(s + 1) * T]

        def dt_block():
            hn_lo = (hn - hn_bf.astype(F32)).astype(BF16)
            p2 = jnp.dot(hn_bf, wdt_ref[...], preferred_element_type=F32)
            dt_raw = p2[:, :LANES] + p2[:, LANES:] + jnp.dot(hn_lo, wdt_ref[:, :LANES],
                                                             preferred_element_type=F32)
            dt_nat = _softplus(dt_raw + dtb_ref[...])
            dta_nat = dt_nat * (-jnp.exp(alog_ref[...]))
            la = _sum3(jnp.dot(cums_ref[...], jnp.concatenate(_split3(dta_nat), axis=1),
                               preferred_element_type=F32))
            dt = _sum3(jnp.dot(perm_ref[...], jnp.concatenate(_split3(dt_nat), axis=1),
                               preferred_element_type=F32))
            la_scr[w] = la
            dt_scr[w] = dt
            laT_scr[w] = la.T
            dtT_scr[w] = dt.T

        tok_t = _chunk_token(lax.broadcasted_iota(jnp.int32, (q, HEADS_PER_GROUP * q), 0), q)
        tok_s = _chunk_token(lax.broadcasted_iota(jnp.int32, (q, HEADS_PER_GROUP * q), 1), q)
        causal = tok_t >= tok_s
        lane_cat = lax.broadcasted_iota(jnp.int32, (q, HEADS_PER_GROUP * q), 1)
        lane_g = lax.broadcasted_iota(jnp.int32, (q, GROUP_COLS), 1)
        head_masks = [(lane_g >= SSM_HEAD_DIM * hr) & (lane_g < SSM_HEAD_DIM * (hr + 1))
                      for hr in range(HEADS_PER_GROUP)]
        sub_iota = lax.broadcasted_iota(jnp.int32, (SUBLANES, CONV_DIM), 0)

        def mix_chunk(s, c):
            base = c * q
            g0 = s * T + c * q

            def halo(v, sh):
                cur = xbc_scr[r, s, base + SUBLANES * v:base + SUBLANES * (v + 1), :]
                if c == 0:
                    prv = before_scr[s, SUBLANES * v:SUBLANES * (v + 1), :]
                else:
                    prv = xbc_scr[r, s, base - q + SUBLANES * v:base - q + SUBLANES * (v + 1), :]
                return jnp.where(sub_iota >= sh, pltpu.roll(cur, sh, 0), pltpu.roll(prv, sh, 0))

            acc = convb_ref[...] + convw_ref[CONV_WIDTH - 1:CONV_WIDTH, :] * xbc_scr[r, s, base:base + q, :]
            for d in range(1, CONV_WIDTH):
                pieces = []
                for j in range(min(d, nv)):
                    jj = j - d
                    pieces.append(halo(jj % nv, -(jj // nv)))
                if q - SUBLANES * d > 0:
                    pieces.append(xbc_scr[r, s, base:base + q - SUBLANES * d, :])
                src = pieces[0] if len(pieces) == 1 else jnp.concatenate(pieces, axis=0)
                acc = acc + convw_ref[CONV_WIDTH - 1 - d:CONV_WIDTH - d, :] * src
            xc = _silu(acc)
            xs = xc[:, :D_INNER]
            b_bf = xc[:, D_INNER:D_INNER + 1024].astype(BF16)
            c_bf = xc[:, D_INNER + 1024:].astype(BF16)

            la_c = la_scr[r, g0:g0 + q, :]
            dt_c = dt_scr[r, g0:g0 + q, :]
            laT_c = laT_scr[r, :, g0:g0 + q]
            dtT_c = dtT_scr[r, :, g0:g0 + q]
            last = la_c[q - 1:q, :]
            e_la = _expand_heads(jnp.exp(la_c))
            w_in = _expand_heads(jnp.exp(last - la_c) * dt_c)
            e_last = _expand_heads(jnp.exp(last))
            xs_bf = xs.astype(BF16)
            xw_bf = (xs * w_in).astype(BF16)
            if q == SSM_HEAD_DIM:
                la_cols = _expand_heads(la_c)

            ys = []
            for g in range(SSM_GROUPS):
                gs = slice(g * GROUP_COLS, (g + 1) * GROUP_COLS)
                h0 = HEADS_PER_GROUP * g
                bg = b_bf[:, g * SSM_STATE:(g + 1) * SSM_STATE]
                cg = c_bf[:, g * SSM_STATE:(g + 1) * SSM_STATE]
                cb = lax.dot_general(cg, jnp.concatenate([bg] * HEADS_PER_GROUP, axis=0), NT_DIMS,
                                     preferred_element_type=F32)
                if q == SSM_HEAD_DIM:
                    la_col = la_cols[:, gs]
                else:
                    la_col = la_c[:, h0 + HEADS_PER_GROUP - 1:h0 + HEADS_PER_GROUP]
                    for hr in range(HEADS_PER_GROUP - 2, -1, -1):
                        la_col = jnp.where(lane_cat < (hr + 1) * q, la_c[:, h0 + hr:h0 + hr + 1], la_col)
                la_row = jnp.concatenate([laT_c[h0 + hr:h0 + hr + 1, :] for hr in range(HEADS_PER_GROUP)], axis=1)
                dt_row = jnp.concatenate([dtT_c[h0 + hr:h0 + hr + 1, :] for hr in range(HEADS_PER_GROUP)], axis=1)
                dec = jnp.exp(jnp.where(causal, la_col - la_row, NEG_BIG))
                m_cat = (cb * dec * dt_row).astype(BF16)
                xg = xs_bf[:, gs]
                x_bd = jnp.concatenate([jnp.where(head_masks[hr], xg, jnp.zeros_like(xg))
                                        for hr in range(HEADS_PER_GROUP)], axis=0)
                y_in = jnp.dot(m_cat, x_bd, preferred_element_type=F32)
                hT = h_scr[s, g]
                y_st = jnp.dot(cg, hT.astype(BF16), preferred_element_type=F32)
                ys.append(y_in + y_st * e_la[:, gs])
                upd = lax.dot_general(bg, xw_bf[:, gs], TN_DIMS, preferred_element_type=F32)
                h_scr[s, g] = hT * e_last[:, gs] + upd

            y = jnp.concatenate(ys, axis=1) + dexp_ref[...] * xs
            gated = y * _silu(z_scr[r, g0:g0 + q, :])
            parts = []
            for k in range(D_INNER // NORM_GROUP):
                gk = gated[:, k * NORM_GROUP:(k + 1) * NORM_GROUP]
                parts.append(gk * lax.rsqrt(jnp.mean(gk * gk, axis=-1, keepdims=True) + EPS))
            y_scr[g0:g0 + q, :] = (jnp.concatenate(parts, axis=1) * normw_ref[...]).astype(BF16)

        def mix_finish():
            for s in range(S):
                before_scr[s] = xbc_scr[r, s, T - q:T, :]
            y_nat = jnp.dot(permT_ref[...], y_scr[...], preferred_element_type=F32).astype(BF16)
            out = jnp.dot(y_nat, wout_ref[...], preferred_element_type=F32) + xres_scr[r]
            out_ref[...] = out.reshape(S, T, D_MODEL)

            @pl.when(last_of_stream)
            def _emit_state():
                for s in range(S):
                    for m, rho in enumerate(tail_rows):
                        convn_ref[s, m:m + 1, :] = before_scr[s, rho:rho + 1, :]
                    for g in range(SSM_GROUPS):
                        hg = h_scr[s, g].T.reshape(HEADS_PER_GROUP, SSM_HEAD_DIM, SSM_STATE)
                        ssmn_ref[s, HEADS_PER_GROUP * g:HEADS_PER_GROUP * (g + 1)] = hg

        proj_items = [functools.partial(proj_block, n) for n in range(ZXBC // PROJ_BLOCK)] + [dt_block]
        mix_items = [functools.partial(mix_chunk, s, c) for s in range(S) for c in range(n_chunks)]
        proj_items[0]()
        rest = np.array_split(np.arange(1, len(proj_items)), len(mix_items))
        for mix, idxs in zip(mix_items, rest):
            mix()
            for k in idxs:
                proj_items[k]()
        mix_finish()

    @pl.when(i % 2 == 0)
    def _even():
        step(0, 1)

    @pl.when(i % 2 == 1)
    def _odd():
        step(1, 0)


def _const_spec(shape):
    zeros = (0,) * len(shape)
    return pl.BlockSpec(shape, lambda *_: zeros, pipeline_mode=pl.Buffered(1))


def _row_order_matrices(M, q):
    rows = np.arange(M)
    chunk = rows // q
    tok_of_row = _chunk_token(rows, q)
    same = chunk[:, None] == chunk[None, :]
    perm = same & (tok_of_row[:, None] == (rows % q)[None, :])
    cums = same & (tok_of_row[:, None] >= (rows % q)[None, :])
    as_bf = lambda m: jnp.asarray(m, dtype=BF16)
    return as_bf(perm), as_bf(perm.T), as_bf(cums)


def _ssd_layer(x, conv0, ssm0, rms_w, w_zxbc, w_dt, conv_w, conv_b, dtb, alog, d_exp, norm_w, w_out, *, S, T, q):
    nb, seq, _ = x.shape
    assert nb % S == 0 and seq % T == 0 and T % q == 0 and q % SUBLANES == 0
    M = S * T
    nt = seq // T
    n_tiles = (nb // S) * nt
    kern = functools.partial(_ssd_kernel, S=S, T=T, q=q, nt=nt)

    def proj_tile(i):
        ia = jnp.minimum(i, n_tiles - 1)
        return ia // nt, ia % nt

    def mix_tile(i):
        ib = jnp.maximum(i - 1, 0)
        return ib // nt, ib % nt

    in_specs = [
        pl.BlockSpec((S, T, D_MODEL), lambda i: (*proj_tile(i), 0)),
        _const_spec((1, D_MODEL)),
        _const_spec((D_MODEL, ZXBC)),
        _const_spec((D_MODEL, 2 * LANES)),
        _const_spec((M, M)),
        _const_spec((M, M)),
        _const_spec((M, M)),
        _const_spec((CONV_WIDTH, CONV_DIM)),
        _const_spec((1, CONV_DIM)),
        _const_spec((1, LANES)),
        _const_spec((1, LANES)),
        _const_spec((1, D_INNER)),
        _const_spec((1, D_INNER)),
        _const_spec((D_INNER, D_MODEL)),
        pl.BlockSpec((S, CONV_WIDTH - 1, CONV_DIM), lambda i: (mix_tile(i)[0], 0, 0)),
        pl.BlockSpec((S, SSM_HEADS, SSM_HEAD_DIM, SSM_STATE), lambda i: (mix_tile(i)[0], 0, 0, 0)),
    ]
    out_specs = [
        pl.BlockSpec((S, T, D_MODEL), lambda i: (*mix_tile(i), 0)),
        pl.BlockSpec((S, CONV_WIDTH - 1, CONV_DIM), lambda i: (mix_tile(i)[0], 0, 0)),
        pl.BlockSpec((S, SSM_HEADS, SSM_HEAD_DIM, SSM_STATE), lambda i: (mix_tile(i)[0], 0, 0, 0)),
    ]
    out_shape = [
        jax.ShapeDtypeStruct((nb, seq, D_MODEL), F32),
        jax.ShapeDtypeStruct((nb, CONV_WIDTH - 1, CONV_DIM), F32),
        jax.ShapeDtypeStruct((nb, SSM_HEADS, SSM_HEAD_DIM, SSM_STATE), F32),
    ]
    scratch = [
        pltpu.VMEM((2, S, T, CONV_DIM), F32),
        pltpu.VMEM((S, q, CONV_DIM), F32),
        pltpu.VMEM((2, M, D_INNER), F32),
        pltpu.VMEM((2, M, D_MODEL), F32),
        pltpu.VMEM((2, M, LANES), F32),
        pltpu.VMEM((2, M, LANES), F32),
        pltpu.VMEM((2, LANES, M), F32),
        pltpu.VMEM((2, LANES, M), F32),
        pltpu.VMEM((M, D_INNER), BF16),
        pltpu.VMEM((S, SSM_GROUPS, SSM_STATE, GROUP_COLS), F32),
    ]
    return pl.pallas_call(
        kern,
        grid=(n_tiles + 1,),
        in_specs=in_specs,
        out_specs=out_specs,
        out_shape=out_shape,
        scratch_shapes=scratch,
        compiler_params=pltpu.CompilerParams(
            dimension_semantics=("arbitrary",),
            vmem_limit_bytes=VMEM_LIMIT_BYTES),
        name=f"ssd_layer_T{T}",
    )(x, rms_w, w_zxbc, w_dt, *_row_order_matrices(M, q), conv_w, conv_b, dtb, alog, d_exp, norm_w, w_out,
      conv0, ssm0)


def _gmlp_kernel(x_ref, rmsw_ref, win_ref, lnw_ref, lnb_ref, wsp_ref, bexp_ref, wout_ref, fw_ref,
                 *refs, S, T, q, emit_v):
    if emit_v:
        out_ref, v_ref, proj_scr, y_scr = refs
    else:
        out_ref, proj_scr, y_scr = refs
        v_ref = None
    M = S * T
    x = x_ref[...].reshape(M, D_MODEL)
    hn_bf = _rms(x, rmsw_ref[...]).astype(BF16)
    for n in range(3 * G_WIDTH // PROJ_BLOCK):
        cols = slice(n * PROJ_BLOCK, (n + 1) * PROJ_BLOCK)
        proj_scr[:, cols] = jnp.dot(hn_bf, win_ref[:, cols], preferred_element_type=F32)

    row = lax.broadcasted_iota(jnp.int32, (q, q), 0)
    col = lax.broadcasted_iota(jnp.int32, (q, q), 1)
    block_causal = (col // CAUSAL_CHUNK) <= (row // CAUSAL_CHUNK)
    wms = [jnp.where(block_causal, wsp_ref[g, 0:q, 0:q], 0.0).astype(BF16) for g in range(G_GROUPS)]

    for s in range(S):
        for c in range(T // q):
            g0 = s * T + c * q
            u = _gelu(proj_scr[g0:g0 + q, 0:G_WIDTH])
            v = _gelu(proj_scr[g0:g0 + q, G_WIDTH:2 * G_WIDTH])
            vc = v - jnp.mean(v, axis=-1, keepdims=True)
            vn = vc * lax.rsqrt(jnp.mean(vc * vc, axis=-1, keepdims=True) + EPS) * lnw_ref[...] + lnb_ref[...]
            if emit_v:
                v_ref[s, c * q:(c + 1) * q, :] = vn
            vn_bf = vn.astype(BF16)
            parts = [jnp.dot(wms[g], vn_bf[:, g * G_GROUP_DIM:(g + 1) * G_GROUP_DIM],
                             preferred_element_type=F32) for g in range(G_GROUPS)]
            sp = jnp.concatenate(parts, axis=1) + bexp_ref[0:q, :]
            y = u * sp * _silu(proj_scr[g0:g0 + q, 2 * G_WIDTH:3 * G_WIDTH])
            y_scr[g0:g0 + q, :] = y.astype(BF16)

    out = jnp.dot(y_scr[...], wout_ref[...], preferred_element_type=F32) + x
    out_ref[...] = _rms(out, fw_ref[...]).reshape(S, T, D_MODEL)


def _gmlp_layer(x, rms_w, w_in, ln_w, ln_b, w_sp, b_exp, w_out, final_w, *, S, T, q, emit_v):
    nb, seq, _ = x.shape
    assert nb % S == 0 and seq % T == 0 and T % q == 0
    M = S * T
    kern = functools.partial(_gmlp_kernel, S=S, T=T, q=q, emit_v=emit_v)
    in_specs = [
        pl.BlockSpec((S, T, D_MODEL), lambda b, t: (b, t, 0)),
        _const_spec((1, D_MODEL)),
        _const_spec((D_MODEL, 3 * G_WIDTH)),
        _const_spec((1, G_WIDTH)),
        _const_spec((1, G_WIDTH)),
        _const_spec((G_GROUPS, G_CHUNK, G_CHUNK)),
        _const_spec((G_CHUNK, G_WIDTH)),
        _const_spec((G_WIDTH, D_MODEL)),
        _const_spec((1, D_MODEL)),
    ]
    out_specs = [pl.BlockSpec((S, T, D_MODEL), lambda b, t: (b, t, 0))]
    out_shape = [jax.ShapeDtypeStruct((nb, seq, D_MODEL), F32)]
    if emit_v:
        out_specs.append(pl.BlockSpec((S, T, G_WIDTH), lambda b, t: (b, t, 0)))
        out_shape.append(jax.ShapeDtypeStruct((nb, seq, G_WIDTH), F32))
    scratch = [pltpu.VMEM((M, 3 * G_WIDTH), F32), pltpu.VMEM((M, G_WIDTH), BF16)]
    return pl.pallas_call(
        kern,
        grid=(nb // S, seq // T),
        in_specs=in_specs,
        out_specs=out_specs,
        out_shape=out_shape,
        scratch_shapes=scratch,
        compiler_params=pltpu.CompilerParams(
            dimension_semantics=("arbitrary", "arbitrary"),
            vmem_limit_bytes=VMEM_LIMIT_BYTES),
        name=f"gmlp_layer_T{T}",
    )(x, rms_w, w_in, ln_w, ln_b, w_sp, b_exp, w_out, final_w)


def _tiling(seq, chunk, max_tile):
    return min(seq, max_tile), min(chunk, seq)


def kernel(x_prompt, x_sample, cache_conv, state_ssm, norm_w, final_norm_w, a_w_in, a_conv_w, a_conv_b,
           a_dt_bias, a_log, a_d, a_norm_w, a_w_out, b_w_in, b_ln_w, b_ln_b, b_w_sp, b_b_sp, b_w_out):
    assert a_w_in.shape[0] == 1 and b_w_in.shape[0] == 1 and norm_w.shape[0] == 2

    w_in = a_w_in.reshape(D_MODEL, ZXBC + SSM_HEADS)
    w_dt = jnp.pad(w_in[:, ZXBC:], ((0, 0), (0, LANES - SSM_HEADS)))
    w_dt_hi = w_dt.astype(BF16)
    w_dt_lo = (w_dt - w_dt_hi.astype(F32)).astype(BF16)
    pad_heads = lambda v: jnp.pad(v.reshape(1, SSM_HEADS), ((0, 0), (0, LANES - SSM_HEADS)))
    ssd_params = dict(
        rms_w=norm_w[0].reshape(1, D_MODEL), w_zxbc=w_in.astype(BF16),
        w_dt=jnp.concatenate([w_dt_hi, w_dt_lo], axis=1),
        conv_w=a_conv_w.reshape(CONV_WIDTH, CONV_DIM), conv_b=a_conv_b.reshape(1, CONV_DIM),
        dtb=pad_heads(a_dt_bias), alog=pad_heads(a_log),
        d_exp=jnp.repeat(a_d.reshape(SSM_HEADS), SSM_HEAD_DIM).reshape(1, D_INNER),
        norm_w=a_norm_w.reshape(1, D_INNER), w_out=a_w_out.reshape(D_INNER, D_MODEL).astype(BF16))
    gmlp_params = dict(
        rms_w=norm_w[1].reshape(1, D_MODEL), w_in=b_w_in.reshape(D_MODEL, 3 * G_WIDTH).astype(BF16),
        ln_w=b_ln_w.reshape(1, G_WIDTH), ln_b=b_ln_b.reshape(1, G_WIDTH),
        w_sp=b_w_sp.reshape(G_GROUPS, G_CHUNK, G_CHUNK),
        b_exp=jnp.repeat(b_b_sp.reshape(G_GROUPS, G_CHUNK).T, G_GROUP_DIM, axis=1),
        w_out=b_w_out.reshape(G_WIDTH, D_MODEL).astype(BF16), final_w=final_norm_w.reshape(1, D_MODEL))

    bp, lp, _ = x_prompt.shape
    bs, ls, _ = x_sample.shape

    Tp, qp = _tiling(lp, CAUSAL_CHUNK, SSD_TILE)
    conv0 = jnp.zeros((bp, CONV_WIDTH - 1, CONV_DIM), F32)
    ssm0 = jnp.zeros((bp, SSM_HEADS, SSM_HEAD_DIM, SSM_STATE), F32)
    hp, conv_p, ssm_p = _ssd_layer(x_prompt, conv0, ssm0, **ssd_params, S=1, T=Tp, q=qp)
    Tg, qg = _tiling(lp, G_CHUNK, GMLP_TILE)
    (y_prompt,) = _gmlp_layer(hp, **gmlp_params, S=1, T=Tg, q=qg, emit_v=False)

    Ts, qs = _tiling(ls, CAUSAL_CHUNK, SSD_TILE)
    hs, conv_s, ssm_s = _ssd_layer(x_sample, cache_conv.reshape(bs, CONV_WIDTH - 1, CONV_DIM),
                                   state_ssm.reshape(bs, SSM_HEADS, SSM_HEAD_DIM, SSM_STATE),
                                   **ssd_params, S=math.gcd(bs, SSD_SAMPLE_STREAMS), T=Ts, q=qs)
    Tg, qg = _tiling(ls, G_CHUNK, GMLP_TILE)
    y_sample, v_s = _gmlp_layer(hs, **gmlp_params, S=math.gcd(bs, GMLP_SAMPLE_STREAMS), T=Tg, q=qg, emit_v=True)

    lead = lambda a: a.reshape((1,) + a.shape)
    return (y_prompt, y_sample, lead(conv_p), lead(ssm_p), lead(conv_s), lead(ssm_s), lead(v_s))
```

```python
import functools
import math

import numpy as np
import jax
import jax.numpy as jnp
from jax import lax
from jax.experimental import pallas as pl
from jax.experimental.pallas import tpu as pltpu

F32 = jnp.float32
BF16 = jnp.bfloat16

EPS = 1e-5
D_MODEL = 1024
D_INNER = 2048
SSM_HEADS = 32
SSM_HEAD_DIM = 64
SSM_STATE = 128
SSM_GROUPS = 8
HEADS_PER_GROUP = 4
GROUP_COLS = HEADS_PER_GROUP * SSM_HEAD_DIM
CONV_WIDTH = 4
CONV_DIM = 4096
NORM_GROUP = 256
G_WIDTH = 2048
G_GROUPS = 8
G_GROUP_DIM = 256
G_CHUNK = 128
CAUSAL_CHUNK = 64
ZXBC = D_INNER + CONV_DIM
PROJ_BLOCK = 1024
SSD_TILE = 256
GMLP_TILE = 512
SSD_SAMPLE_STREAMS = 4
GMLP_SAMPLE_STREAMS = 32
LANES = 128
SUBLANES = 8
VMEM_LIMIT_BYTES = 58 * 1024 * 1024
NEG_BIG = -1e30

NT_DIMS = (((1,), (1,)), ((), ()))
TN_DIMS = (((0,), (0,)), ((), ()))


def _silu(x):
    return x / (1.0 + jnp.exp(-x))


def _softplus(x):
    return jnp.maximum(x, 0.0) + jnp.log1p(jnp.exp(-jnp.abs(x)))


def _gelu(x):
    return 0.5 * x * (1.0 + lax.erf(x * 0.7071067811865476))


def _rms(x, w):
    return x * lax.rsqrt(jnp.mean(x * x, axis=-1, keepdims=True) + EPS) * w


def _split3(v):
    hi = v.astype(BF16)
    r1 = v - hi.astype(F32)
    mid = r1.astype(BF16)
    lo = (r1 - mid.astype(F32)).astype(BF16)
    return hi, mid, lo


def _sum3(p):
    return p[:, :LANES] + (p[:, LANES:2 * LANES] + p[:, 2 * LANES:])


def _expand_heads(v):
    lane = lax.broadcasted_iota(jnp.int32, (v.shape[0], LANES), 1)
    first = lane < SSM_HEAD_DIM
    pieces = [jnp.where(first, v[:, 2 * j:2 * j + 1], v[:, 2 * j + 1:2 * j + 2])
              for j in range(SSM_HEADS // 2)]
    return jnp.concatenate(pieces, axis=1)


def _chunk_token(row, q):
    r = row % q
    return r // SUBLANES + (q // SUBLANES) * (r % SUBLANES)


def _token_row(tau, q):
    nv = q // SUBLANES
    return SUBLANES * (tau % nv) + tau // nv


def _ssd_kernel(x_ref, rmsw_ref, win_ref, wdt_ref, perm_ref, permT_ref, cums_ref, convw_ref, convb_ref,
                dtb_ref, alog_ref, dexp_ref, normw_ref, wout_ref, conv0_ref, ssm0_ref,
                out_ref, convn_ref, ssmn_ref,
                xbc_scr, before_scr, z_scr, xres_scr, la_scr, dt_scr, laT_scr, dtT_scr, y_scr, h_scr,
                *, S, T, q, nt):
    i = pl.program_id(0)
    M = S * T
    nv = q // SUBLANES
    n_chunks = T // q
    tail_rows = [_token_row(q - (CONV_WIDTH - 1) + m, q) for m in range(CONV_WIDTH - 1)]
    mix_tile = jnp.maximum(i - 1, 0) % nt
    first_of_stream = (i >= 1) & (mix_tile == 0)
    last_of_stream = (i >= 1) & (mix_tile == nt - 1)

    @pl.when(i == 0)
    def _prime():
        xbc_scr[1] = jnp.zeros((S, T, CONV_DIM), F32)
        z_scr[1] = jnp.zeros((M, D_INNER), F32)
        xres_scr[1] = jnp.zeros((M, D_MODEL), F32)
        la_scr[1] = jnp.zeros((M, LANES), F32)
        dt_scr[1] = jnp.zeros((M, LANES), F32)
        laT_scr[1] = jnp.zeros((LANES, M), F32)
        dtT_scr[1] = jnp.zeros((LANES, M), F32)
        before_scr[...] = jnp.zeros((S, q, CONV_DIM), F32)
        h_scr[...] = jnp.zeros((S, SSM_GROUPS, SSM_STATE, GROUP_COLS), F32)

    @pl.when(first_of_stream)
    def _load_state():
        for s in range(S):
            before_scr[s] = jnp.zeros((q, CONV_DIM), F32)
            for m, rho in enumerate(tail_rows):
                before_scr[s, rho:rho + 1, :] = conv0_ref[s, m:m + 1, :]
            for g in range(SSM_GROUPS):
                hg = ssm0_ref[s, HEADS_PER_GROUP * g:HEADS_PER_GROUP * (g + 1)]
                h_scr[s, g] = hg.reshape(GROUP_COLS, SSM_STATE).T

    def step(w, r):
        x = x_ref[...].reshape(M, D_MODEL)
        xres_scr[w] = x
        hn = _rms(x, rmsw_ref[...])
        hn_bf = hn.astype(BF16)
        hn_p = jnp.dot(perm_ref[...], hn_bf, preferred_element_type=F32).astype(BF16)

        def proj_block(n):
            blk = jnp.dot(hn_p, win_ref[:, n * PROJ_BLOCK:(n + 1) * PROJ_BLOCK], preferred_element_type=F32)
            c0 = n * PROJ_BLOCK
            if c0 < D_INNER:
                z_scr[w, :, c0:c0 + PROJ_BLOCK] = blk
            else:
                for s in range(S):
                    xbc_scr[w, s, :, c0 - D_INNER:c0 - D_INNER + PROJ_BLOCK] = blk[s * T:(s + 1) * T]

        def dt_block():
            hn_lo = (hn - hn_bf.astype(F32)).astype(BF16)
            p2 = jnp.dot(hn_bf, wdt_ref[...], preferred_element_type=F32)
            dt_raw = p2[:, :LANES] + p2[:, LANES:] + jnp.dot(hn_lo, wdt_ref[:, :LANES],
                                                             preferred_element_type=F32)
            dt_nat = _softplus(dt_raw + dtb_ref[...])
            dta_nat = dt_nat * (-jnp.exp(alog_ref[...]))
            la = _sum3(jnp.dot(cums_ref[...], jnp.concatenate(_split3(dta_nat), axis=1),
                               preferred_element_type=F32))
            dt = _sum3(jnp.dot(perm_ref[...], jnp.concatenate(_split3(dt_nat), axis=1),
                               preferred_element_type=F32))
            la_scr[w] = la
            dt_scr[w] = dt
            laT_scr[w] = la.T
            dtT_scr[w] = dt.T

        tok_t = _chunk_token(lax.broadcasted_iota(jnp.int32, (q, HEADS_PER_GROUP * q), 0), q)
        tok_s = _chunk_token(lax.broadcasted_iota(jnp.int32, (q, HEADS_PER_GROUP * q), 1), q)
        causal = tok_t >= tok_s
        lane_cat = lax.broadcasted_iota(jnp.int32, (q, HEADS_PER_GROUP * q), 1)
        lane_g = lax.broadcasted_iota(jnp.int32, (q, GROUP_COLS), 1)
        head_masks = [(lane_g >= SSM_HEAD_DIM * hr) & (lane_g < SSM_HEAD_DIM * (hr + 1))
                      for hr in range(HEADS_PER_GROUP)]
        sub_iota = lax.broadcasted_iota(jnp.int32, (SUBLANES, CONV_DIM), 0)

        def mix_chunk(s, c):
            base = c * q
            g0 = s * T + c * q

            def halo(v, sh):
                cur = xbc_scr[r, s, base + SUBLANES * v:base + SUBLANES * (v + 1), :]
                if c == 0:
                    prv = before_scr[s, SUBLANES * v:SUBLANES * (v + 1), :]
                else:
                    prv = xbc_scr[r, s, base - q + SUBLANES * v:base - q + SUBLANES * (v + 1), :]
                return jnp.where(sub_iota >= sh, pltpu.roll(cur, sh, 0), pltpu.roll(prv, sh, 0))

            acc = convb_ref[...] + convw_ref[CONV_WIDTH - 1:CONV_WIDTH, :] * xbc_scr[r, s, base:base + q, :]
            for d in range(1, CONV_WIDTH):
                pieces = []
                for j in range(min(d, nv)):
                    jj = j - d
                    pieces.append(halo(jj % nv, -(jj // nv)))
                if q - SUBLANES * d > 0:
                    pieces.append(xbc_scr[r, s, base:base + q - SUBLANES * d, :])
                src = pieces[0] if len(pieces) == 1 else jnp.concatenate(pieces, axis=0)
                acc = acc + convw_ref[CONV_WIDTH - 1 - d:CONV_WIDTH - d, :] * src
            xc = _silu(acc.astype(BF16))
            xs_bf = xc[:, :D_INNER]
            xs = xs_bf.astype(F32)
            b_bf = xc[:, D_INNER:D_INNER + 1024]
            c_bf = xc[:, D_INNER + 1024:]

            la_c = la_scr[r, g0:g0 + q, :]
            dt_c = dt_scr[r, g0:g0 + q, :]
            laT_c = laT_scr[r, :, g0:g0 + q]
            dtT_c = dtT_scr[r, :, g0:g0 + q]
            last = la_c[q - 1:q, :]
            e_la = _expand_heads(jnp.exp(la_c))
            w_in = _expand_heads(jnp.exp(last - la_c) * dt_c)
            e_last = _expand_heads(jnp.exp(last))
            xw_bf = (xs * w_in).astype(BF16)
            if q == SSM_HEAD_DIM:
                la_cols = _expand_heads(la_c)

            ys = []
            for g in range(SSM_GROUPS):
                gs = slice(g * GROUP_COLS, (g + 1) * GROUP_COLS)
                h0 = HEADS_PER_GROUP * g
                bg = b_bf[:, g * SSM_STATE:(g + 1) * SSM_STATE]
                cg = c_bf[:, g * SSM_STATE:(g + 1) * SSM_STATE]
                cb = lax.dot_general(cg, jnp.concatenate([bg] * HEADS_PER_GROUP, axis=0), NT_DIMS,
                                     preferred_element_type=F32)
                if q == SSM_HEAD_DIM:
                    la_col = la_cols[:, gs]
                else:
                    la_col = la_c[:, h0 + HEADS_PER_GROUP - 1:h0 + HEADS_PER_GROUP]
                    for hr in range(HEADS_PER_GROUP - 2, -1, -1):
                        la_col = jnp.where(lane_cat < (hr + 1) * q, la_c[:, h0 + hr:h0 + hr + 1], la_col)
                la_row = jnp.concatenate([laT_c[h0 + hr:h0 + hr + 1, :] for hr in range(HEADS_PER_GROUP)], axis=1)
                dt_row = jnp.concatenate([dtT_c[h0 + hr:h0 + hr + 1, :] for hr in range(HEADS_PER_GROUP)], axis=1)
                dec = jnp.exp(jnp.where(causal, la_col - la_row, NEG_BIG).astype(BF16))
                m_cat = cb.astype(BF16) * dec * dt_row.astype(BF16)
                xg = xs_bf[:, gs]
                x_bd = jnp.concatenate([jnp.where(head_masks[hr], xg, jnp.zeros_like(xg))
                                        for hr in range(HEADS_PER_GROUP)], axis=0)
                y_in = jnp.dot(m_cat, x_bd, preferred_element_type=F32)
                hT = h_scr[s, g]
                y_st = jnp.dot(cg, hT.astype(BF16), preferred_element_type=F32)
                ys.append(y_in + y_st * e_la[:, gs])
                upd = lax.dot_general(bg, xw_bf[:, gs], TN_DIMS, preferred_element_type=F32)
                h_scr[s, g] = hT * e_last[:, gs] + upd

            y = jnp.concatenate(ys, axis=1) + dexp_ref[...] * xs
            gated = y * _silu(z_scr[r, g0:g0 + q, :])
            parts = []
            for k in range(D_INNER // NORM_GROUP):
                gk = gated[:, k * NORM_GROUP:(k + 1) * NORM_GROUP]
                parts.append(gk * lax.rsqrt(jnp.mean(gk * gk, axis=-1, keepdims=True) + EPS))
            y_scr[g0:g0 + q, :] = (jnp.concatenate(parts, axis=1) * normw_ref[...]).astype(BF16)

        def mix_finish():
            for s in range(S):
                before_scr[s] = xbc_scr[r, s, T - q:T, :]
            y_nat = jnp.dot(permT_ref[...], y_scr[...], preferred_element_type=F32).astype(BF16)
            out = jnp.dot(y_nat, wout_ref[...], preferred_element_type=F32) + xres_scr[r]
            out_ref[...] = out.reshape(S, T, D_MODEL)

            @pl.when(last_of_stream)
            def _emit_state():
                for s in range(S):
                    for m, rho in enumerate(tail_rows):
                        convn_ref[s, m:m + 1, :] = before_scr[s, rho:rho + 1, :]
                    for g in range(SSM_GROUPS):
                        hg = h_scr[s, g].T.reshape(HEADS_PER_GROUP, SSM_HEAD_DIM, SSM_STATE)
                        ssmn_ref[s, HEADS_PER_GROUP * g:HEADS_PER_GROUP * (g + 1)] = hg

        proj_items = [functools.partial(proj_block, n) for n in range(ZXBC // PROJ_BLOCK)] + [dt_block]
        mix_items = [functools.partial(mix_chunk, s, c) for s in range(S) for c in range(n_chunks)]
        proj_items[0]()
        rest = np.array_split(np.arange(1, len(proj_items)), len(mix_items))
        for mix, idxs in zip(mix_items, rest):
            mix()
            for k in idxs:
                proj_items[k]()
        mix_finish()

    @pl.when(i % 2 == 0)
    def _even():
        step(0, 1)

    @pl.when(i % 2 == 1)
    def _odd():
        step(1, 0)


def _const_spec(shape):
    zeros = (0,) * len(shape)
    return pl.BlockSpec(shape, lambda *_: zeros, pipeline_mode=pl.Buffered(1))


def _row_order_matrices(M, q):
    rows = np.arange(M)
    chunk = rows // q
    tok_of_row = _chunk_token(rows, q)
    same = chunk[:, None] == chunk[None, :]
    perm = same & (tok_of_row[:, None] == (rows % q)[None, :])
    cums = same & (tok_of_row[:, None] >= (rows % q)[None, :])
    as_bf = lambda m: jnp.asarray(m, dtype=BF16)
    return as_bf(perm), as_bf(perm.T), as_bf(cums)


def _ssd_layer(x, conv0, ssm0, rms_w, w_zxbc, w_dt, conv_w, conv_b, dtb, alog, d_exp, norm_w, w_out, *, S, T, q):
    nb, seq, _ = x.shape
    assert nb % S == 0 and seq % T == 0 and T % q == 0 and q % SUBLANES == 0
    M = S * T
    nt = seq // T
    n_tiles = (nb // S) * nt
    kern = functools.partial(_ssd_kernel, S=S, T=T, q=q, nt=nt)

    def proj_tile(i):
        ia = jnp.minimum(i, n_tiles - 1)
        return ia // nt, ia % nt

    def mix_tile(i):
        ib = jnp.maximum(i - 1, 0)
        return ib // nt, ib % nt

    in_specs = [
        pl.BlockSpec((S, T, D_MODEL), lambda i: (*proj_tile(i), 0)),
        _const_spec((1, D_MODEL)),
        _const_spec((D_MODEL, ZXBC)),
        _const_spec((D_MODEL, 2 * LANES)),
        _const_spec((M, M)),
        _const_spec((M, M)),
        _const_spec((M, M)),
        _const_spec((CONV_WIDTH, CONV_DIM)),
        _const_spec((1, CONV_DIM)),
        _const_spec((1, LANES)),
        _const_spec((1, LANES)),
        _const_spec((1, D_INNER)),
        _const_spec((1, D_INNER)),
        _const_spec((D_INNER, D_MODEL)),
        pl.BlockSpec((S, CONV_WIDTH - 1, CONV_DIM), lambda i: (mix_tile(i)[0], 0, 0)),
        pl.BlockSpec((S, SSM_HEADS, SSM_HEAD_DIM, SSM_STATE), lambda i: (mix_tile(i)[0], 0, 0, 0)),
    ]
    out_specs = [
        pl.BlockSpec((S, T, D_MODEL), lambda i: (*mix_tile(i), 0)),
        pl.BlockSpec((S, CONV_WIDTH - 1, CONV_DIM), lambda i: (mix_tile(i)[0], 0, 0)),
        pl.BlockSpec((S, SSM_HEADS, SSM_HEAD_DIM, SSM_STATE), lambda i: (mix_tile(i)[0], 0, 0, 0)),
    ]
    out_shape = [
        jax.ShapeDtypeStruct((nb, seq, D_MODEL), F32),
        jax.ShapeDtypeStruct((nb, CONV_WIDTH - 1, CONV_DIM), F32),
        jax.ShapeDtypeStruct((nb, SSM_HEADS, SSM_HEAD_DIM, SSM_STATE), F32),
    ]
    scratch = [
        pltpu.VMEM((2, S, T, CONV_DIM), F32),
        pltpu.VMEM((S, q, CONV_DIM), F32),
        pltpu.VMEM((2, M, D_INNER), F32),
        pltpu.VMEM((2, M, D_MODEL), F32),
        pltpu.VMEM((2, M, LANES), F32),
        pltpu.VMEM((2, M, LANES), F32),
        pltpu.VMEM((2, LANES, M), F32),
        pltpu.VMEM((2, LANES, M), F32),
        pltpu.VMEM((M, D_INNER), BF16),
        pltpu.VMEM((S, SSM_GROUPS, SSM_STATE, GROUP_COLS), F32),
    ]
    return pl.pallas_call(
        kern,
        grid=(n_tiles + 1,),
        in_specs=in_specs,
        out_specs=out_specs,
        out_shape=out_shape,
        scratch_shapes=scratch,
        compiler_params=pltpu.CompilerParams(
            dimension_semantics=("arbitrary",),
            vmem_limit_bytes=VMEM_LIMIT_BYTES),
        name=f"ssd_layer_T{T}",
    )(x, rms_w, w_zxbc, w_dt, *_row_order_matrices(M, q), conv_w, conv_b, dtb, alog, d_exp, norm_w, w_out,
      conv0, ssm0)


def _gmlp_kernel(x_ref, rmsw_ref, win_ref, lnw_ref, lnb_ref, wsp_ref, bexp_ref, wout_ref, fw_ref,
                 *refs, S, T, q, emit_v):
    if emit_v:
        out_ref, v_ref, proj_scr, y_scr = refs
    else:
        out_ref, proj_scr, y_scr = refs
        v_ref = None
    M = S * T
    x = x_ref[...].reshape(M, D_MODEL)
    hn_bf = _rms(x, rmsw_ref[...]).astype(BF16)
    for n in range(3 * G_WIDTH // PROJ_BLOCK):
        cols = slice(n * PROJ_BLOCK, (n + 1) * PROJ_BLOCK)
        proj_scr[:, cols] = jnp.dot(hn_bf, win_ref[:, cols], preferred_element_type=F32)

    row = lax.broadcasted_iota(jnp.int32, (q, q), 0)
    col = lax.broadcasted_iota(jnp.int32, (q, q), 1)
    block_causal = (col // CAUSAL_CHUNK) <= (row // CAUSAL_CHUNK)
    wms = [jnp.where(block_causal, wsp_ref[g, 0:q, 0:q], 0.0).astype(BF16) for g in range(G_GROUPS)]

    for s in range(S):
        for c in range(T // q):
            g0 = s * T + c * q
            u = _gelu(proj_scr[g0:g0 + q, 0:G_WIDTH])
            v = _gelu(proj_scr[g0:g0 + q, G_WIDTH:2 * G_WIDTH])
            vc = v - jnp.mean(v, axis=-1, keepdims=True)
            vn = vc * lax.rsqrt(jnp.mean(vc * vc, axis=-1, keepdims=True) + EPS) * lnw_ref[...] + lnb_ref[...]
            if emit_v:
                v_ref[s, c * q:(c + 1) * q, :] = vn
            vn_bf = vn.astype(BF16)
            parts = [jnp.dot(wms[g], vn_bf[:, g * G_GROUP_DIM:(g + 1) * G_GROUP_DIM],
                             preferred_element_type=F32) for g in range(G_GROUPS)]
            sp = jnp.concatenate(parts, axis=1) + bexp_ref[0:q, :]
            y = u * sp * _silu(proj_scr[g0:g0 + q, 2 * G_WIDTH:3 * G_WIDTH])
            y_scr[g0:g0 + q, :] = y.astype(BF16)

    out = jnp.dot(y_scr[...], wout_ref[...], preferred_element_type=F32) + x
    out_ref[...] = _rms(out, fw_ref[...]).reshape(S, T, D_MODEL)


def _gmlp_layer(x, rms_w, w_in, ln_w, ln_b, w_sp, b_exp, w_out, final_w, *, S, T, q, emit_v):
    nb, seq, _ = x.shape
    assert nb % S == 0 and seq % T == 0 and T % q == 0
    M = S * T
    kern = functools.partial(_gmlp_kernel, S=S, T=T, q=q, emit_v=emit_v)
    in_specs = [
        pl.BlockSpec((S, T, D_MODEL), lambda b, t: (b, t, 0)),
        _const_spec((1, D_MODEL)),
        _const_spec((D_MODEL, 3 * G_WIDTH)),
        _const_spec((1, G_WIDTH)),
        _const_spec((1, G_WIDTH)),
        _const_spec((G_GROUPS, G_CHUNK, G_CHUNK)),
        _const_spec((G_CHUNK, G_WIDTH)),
        _const_spec((G_WIDTH, D_MODEL)),
        _const_spec((1, D_MODEL)),
    ]
    out_specs = [pl.BlockSpec((S, T, D_MODEL), lambda b, t: (b, t, 0))]
    out_shape = [jax.ShapeDtypeStruct((nb, seq, D_MODEL), F32)]
    if emit_v:
        out_specs.append(pl.BlockSpec((S, T, G_WIDTH), lambda b, t: (b, t, 0)))
        out_shape.append(jax.ShapeDtypeStruct((nb, seq, G_WIDTH), F32))
    scratch = [pltpu.VMEM((M, 3 * G_WIDTH), F32), pltpu.VMEM((M, G_WIDTH), BF16)]
    return pl.pallas_call(
        kern,
        grid=(nb // S, seq // T),
        in_specs=in_specs,
        out_specs=out_specs,
        out_shape=out_shape,
        scratch_shapes=scratch,
        compiler_params=pltpu.CompilerParams(
            dimension_semantics=("arbitrary", "arbitrary"),
            vmem_limit_bytes=VMEM_LIMIT_BYTES),
        name=f"gmlp_layer_T{T}",
    )(x, rms_w, w_in, ln_w, ln_b, w_sp, b_exp, w_out, final_w)


def _tiling(seq, chunk, max_tile):
    return min(seq, max_tile), min(chunk, seq)


def kernel(x_prompt, x_sample, cache_conv, state_ssm, norm_w, final_norm_w, a_w_in, a_conv_w, a_conv_b,
           a_dt_bias, a_log, a_d, a_norm_w, a_w_out, b_w_in, b_ln_w, b_ln_b, b_w_sp, b_b_sp, b_w_out):
    assert a_w_in.shape[0] == 1 and b_w_in.shape[0] == 1 and norm_w.shape[0] == 2

    w_in = a_w_in.reshape(D_MODEL, ZXBC + SSM_HEADS)
    w_dt = jnp.pad(w_in[:, ZXBC:], ((0, 0), (0, LANES - SSM_HEADS)))
    w_dt_hi = w_dt.astype(BF16)
    w_dt_lo = (w_dt - w_dt_hi.astype(F32)).astype(BF16)
    pad_heads = lambda v: jnp.pad(v.reshape(1, SSM_HEADS), ((0, 0), (0, LANES - SSM_HEADS)))
    ssd_params = dict(
        rms_w=norm_w[0].reshape(1, D_MODEL),
        w_zxbc=a_w_in.astype(BF16).reshape(D_MODEL, ZXBC + SSM_HEADS),
        w_dt=jnp.concatenate([w_dt_hi, w_dt_lo], axis=1),
        conv_w=a_conv_w.reshape(CONV_WIDTH, CONV_DIM), conv_b=a_conv_b.reshape(1, CONV_DIM),
        dtb=pad_heads(a_dt_bias), alog=pad_heads(a_log),
        d_exp=jnp.repeat(a_d.reshape(SSM_HEADS), SSM_HEAD_DIM).reshape(1, D_INNER),
        norm_w=a_norm_w.reshape(1, D_INNER), w_out=a_w_out.reshape(D_INNER, D_MODEL).astype(BF16))
    gmlp_params = dict(
        rms_w=norm_w[1].reshape(1, D_MODEL), w_in=b_w_in.reshape(D_MODEL, 3 * G_WIDTH).astype(BF16),
        ln_w=b_ln_w.reshape(1, G_WIDTH), ln_b=b_ln_b.reshape(1, G_WIDTH),
        w_sp=b_w_sp.reshape(G_GROUPS, G_CHUNK, G_CHUNK),
        b_exp=jnp.repeat(b_b_sp.reshape(G_GROUPS, G_CHUNK).T, G_GROUP_DIM, axis=1),
        w_out=b_w_out.reshape(G_WIDTH, D_MODEL).astype(BF16), final_w=final_norm_w.reshape(1, D_MODEL))

    bp, lp, _ = x_prompt.shape
    bs, ls, _ = x_sample.shape

    Tp, qp = _tiling(lp, CAUSAL_CHUNK, SSD_TILE)
    conv0 = jnp.zeros((bp, CONV_WIDTH - 1, CONV_DIM), F32)
    ssm0 = jnp.zeros((bp, SSM_HEADS, SSM_HEAD_DIM, SSM_STATE), F32)
    hp, conv_p, ssm_p = _ssd_layer(x_prompt, conv0, ssm0, **ssd_params, S=1, T=Tp, q=qp)
    Tg, qg = _tiling(lp, G_CHUNK, GMLP_TILE)
    (y_prompt,) = _gmlp_layer(hp, **gmlp_params, S=1, T=Tg, q=qg, emit_v=False)

    Ts, qs = _tiling(ls, CAUSAL_CHUNK, SSD_TILE)
    hs, conv_s, ssm_s = _ssd_layer(x_sample, cache_conv.reshape(bs, CONV_WIDTH - 1, CONV_DIM),
                                   state_ssm.reshape(bs, SSM_HEADS, SSM_HEAD_DIM, SSM_STATE),
                                   **ssd_params, S=math.gcd(bs, SSD_SAMPLE_STREAMS), T=Ts, q=qs)
    Tg, qg = _tiling(ls, G_CHUNK, GMLP_TILE)
    y_sample, v_s = _gmlp_layer(hs, **gmlp_params, S=math.gcd(bs, GMLP_SAMPLE_STREAMS), T=Tg, q=qg, emit_v=True)

    lead = lambda a: a.reshape((1,) + a.shape)
    return (y_prompt, y_sample, lead(conv_p), lead(ssm_p), lead(conv_s), lead(ssm_s), lead(v_s))
```

```python
import functools
import math

import numpy as np
import jax
import jax.numpy as jnp
from jax import lax
from jax.experimental import pallas as pl
from jax.experimental.pallas import tpu as pltpu

F32 = jnp.float32
BF16 = jnp.bfloat16

EPS = 1e-5
D_MODEL = 1024
D_INNER = 2048
SSM_HEADS = 32
SSM_HEAD_DIM = 64
SSM_STATE = 128
SSM_GROUPS = 8
HEADS_PER_GROUP = 4
GROUP_COLS = HEADS_PER_GROUP * SSM_HEAD_DIM
CONV_WIDTH = 4
GN = SSM_GROUPS * SSM_STATE
CONV_DIM = D_INNER + 2 * GN
NORM_GROUP = 256
G_WIDTH = 2048
G_GROUPS = 8
G_GROUP_DIM = 256
G_CHUNK = 128
CAUSAL_CHUNK = 64
ZXBC = D_INNER + CONV_DIM
PROJ_BLOCK = 1024
SSD_TILE = 256
GMLP_TILE = 512
SSD_SAMPLE_STREAMS = 4
GMLP_SAMPLE_STREAMS = 32
LANES = 128
SUBLANES = 8
VMEM_LIMIT_BYTES = 58 * 1024 * 1024
NEG_BIG = -1e30

NT_DIMS = (((1,), (1,)), ((), ()))
TN_DIMS = (((0,), (0,)), ((), ()))


def _silu(x):
    return x / (1.0 + jnp.exp(-x))


def _softplus(x):
    return jnp.maximum(x, 0.0) + jnp.log1p(jnp.exp(-jnp.abs(x)))


def _gelu(x):
    return 0.5 * x * (1.0 + lax.erf(x * 0.7071067811865476))


def _rms(x, w):
    return x * lax.rsqrt(jnp.mean(x * x, axis=-1, keepdims=True) + EPS) * w


def _split3(v):
    hi = v.astype(BF16)
    r1 = v - hi.astype(F32)
    mid = r1.astype(BF16)
    lo = (r1 - mid.astype(F32)).astype(BF16)
    return hi, mid, lo


def _sum3(p):
    return p[:, :LANES] + (p[:, LANES:2 * LANES] + p[:, 2 * LANES:])


def _expand_heads(v):
    lane = lax.broadcasted_iota(jnp.int32, (v.shape[0], LANES), 1)
    first = lane < SSM_HEAD_DIM
    pieces = [jnp.where(first, v[:, 2 * j:2 * j + 1], v[:, 2 * j + 1:2 * j + 2])
              for j in range(SSM_HEADS // 2)]
    return jnp.concatenate(pieces, axis=1)


def _chunk_token(row, q):
    r = row % q
    return r // SUBLANES + (q // SUBLANES) * (r % SUBLANES)


def _token_row(tau, q):
    nv = q // SUBLANES
    return SUBLANES * (tau % nv) + tau // nv


def _ssd_kernel(x_ref, rmsw_ref, win_ref, wdt_ref, perm_ref, permT_ref, cums_ref, convw_ref, convb_ref,
                dtb_ref, alog_ref, dexp_ref, normw_ref, wout_ref, *refs, S, T, q, nt, zero_state):
    if zero_state:
        conv0_ref = ssm0_ref = None
    else:
        conv0_ref, ssm0_ref, *refs = refs
    (out_ref, convn_ref, ssmn_ref,
     xbc_scr, before_scr, z_scr, xres_scr, la_scr, dt_scr, laT_scr, dtT_scr, y_scr, h_scr) = refs
    i = pl.program_id(0)
    M = S * T
    nv = q // SUBLANES
    n_chunks = T // q
    tail_rows = [_token_row(q - (CONV_WIDTH - 1) + m, q) for m in range(CONV_WIDTH - 1)]
    mix_tile = jnp.maximum(i - 1, 0) % nt
    first_of_stream = (i >= 1) & (mix_tile == 0)
    last_of_stream = (i >= 1) & (mix_tile == nt - 1)

    @pl.when(i == 0)
    def _prime():
        xbc_scr[1] = jnp.zeros((S, T, CONV_DIM), F32)
        z_scr[1] = jnp.zeros((M, D_INNER), F32)
        xres_scr[1] = jnp.zeros((M, D_MODEL), F32)
        la_scr[1] = jnp.zeros((M, LANES), F32)
        dt_scr[1] = jnp.zeros((M, LANES), F32)
        laT_scr[1] = jnp.zeros((LANES, M), F32)
        dtT_scr[1] = jnp.zeros((LANES, M), F32)
        before_scr[...] = jnp.zeros((S, q, CONV_DIM), F32)
        h_scr[...] = jnp.zeros((S, SSM_GROUPS, SSM_STATE, GROUP_COLS), F32)

    @pl.when(first_of_stream)
    def _load_state():
        for s in range(S):
            before_scr[s] = jnp.zeros((q, CONV_DIM), F32)
            if zero_state:
                h_scr[s] = jnp.zeros((SSM_GROUPS, SSM_STATE, GROUP_COLS), F32)
                continue
            for m, rho in enumerate(tail_rows):
                before_scr[s, rho:rho + 1, :] = conv0_ref[s, m:m + 1, :]
            for g in range(SSM_GROUPS):
                hg = ssm0_ref[s, HEADS_PER_GROUP * g:HEADS_PER_GROUP * (g + 1)]
                h_scr[s, g] = hg.reshape(GROUP_COLS, SSM_STATE).T

    def step(w, r):
        x = x_ref[...].reshape(M, D_MODEL)
        xres_scr[w] = x
        hn = _rms(x, rmsw_ref[...])
        hn_bf = hn.astype(BF16)
        hn_p = jnp.dot(perm_ref[...], hn_bf, preferred_element_type=F32).astype(BF16)

        def proj_block(n):
            blk = jnp.dot(hn_p, win_ref[:, n * PROJ_BLOCK:(n + 1) * PROJ_BLOCK], preferred_element_type=F32)
            c0 = n * PROJ_BLOCK
            if c0 < D_INNER:
                z_scr[w, :, c0:c0 + PROJ_BLOCK] = blk
            else:
                for s in range(S):
                    xbc_scr[w, s, :, c0 - D_INNER:c0 - D_INNER + PROJ_BLOCK] = blk[s * T:(s + 1) * T]

        def dt_block():
            hn_lo = (hn - hn_bf.astype(F32)).astype(BF16)
            p2 = jnp.dot(hn_bf, wdt_ref[...], preferred_element_type=F32)
            dt_raw = p2[:, :LANES] + p2[:, LANES:] + jnp.dot(hn_lo, wdt_ref[:, :LANES],
                                                             preferred_element_type=F32)
            dt_nat = _softplus(dt_raw + dtb_ref[...])
            dta_nat = dt_nat * (-jnp.exp(alog_ref[...]))
            la = _sum3(jnp.dot(cums_ref[...], jnp.concatenate(_split3(dta_nat), axis=1),
                               preferred_element_type=F32))
            dt = _sum3(jnp.dot(perm_ref[...], jnp.concatenate(_split3(dt_nat), axis=1),
                               preferred_element_type=F32))
            la_scr[w] = la
            dt_scr[w] = dt
            laT_scr[w] = la.T
            dtT_scr[w] = dt.T

        tok_t = _chunk_token(lax.broadcasted_iota(jnp.int32, (q, HEADS_PER_GROUP * q), 0), q)
        tok_s = _chunk_token(lax.broadcasted_iota(jnp.int32, (q, HEADS_PER_GROUP * q), 1), q)
        causal = tok_t >= tok_s
        lane_cat = lax.broadcasted_iota(jnp.int32, (q, HEADS_PER_GROUP * q), 1)
        lane_g = lax.broadcasted_iota(jnp.int32, (q, GROUP_COLS), 1)
        head_masks = [(lane_g >= SSM_HEAD_DIM * hr) & (lane_g < SSM_HEAD_DIM * (hr + 1))
                      for hr in range(HEADS_PER_GROUP)]
        sub_iota = lax.broadcasted_iota(jnp.int32, (SUBLANES, CONV_DIM), 0)

        def mix_chunk(s, c):
            base = c * q
            g0 = s * T + c * q

            def halo(v, sh):
                cur = xbc_scr[r, s, base + SUBLANES * v:base + SUBLANES * (v + 1), :]
                if c == 0:
                    prv = before_scr[s, SUBLANES * v:SUBLANES * (v + 1), :]
                else:
                    prv = xbc_scr[r, s, base - q + SUBLANES * v:base - q + SUBLANES * (v + 1), :]
                return jnp.where(sub_iota >= sh, pltpu.roll(cur, sh, 0), pltpu.roll(prv, sh, 0))

            acc = convb_ref[...] + convw_ref[CONV_WIDTH - 1:CONV_WIDTH, :] * xbc_scr[r, s, base:base + q, :]
            for d in range(1, CONV_WIDTH):
                pieces = []
                for j in range(min(d, nv)):
                    jj = j - d
                    pieces.append(halo(jj % nv, -(jj // nv)))
                if q - SUBLANES * d > 0:
                    pieces.append(xbc_scr[r, s, base:base + q - SUBLANES * d, :])
                src = pieces[0] if len(pieces) == 1 else jnp.concatenate(pieces, axis=0)
                acc = acc + convw_ref[CONV_WIDTH - 1 - d:CONV_WIDTH - d, :] * src
            xc = _silu(acc.astype(BF16))
            xs_bf = xc[:, :D_INNER]
            xs = xs_bf.astype(F32)
            b_bf = xc[:, D_INNER:D_INNER + GN]
            c_bf = xc[:, D_INNER + GN:]

            la_c = la_scr[r, g0:g0 + q, :]
            dt_c = dt_scr[r, g0:g0 + q, :]
            laT_c = laT_scr[r, :, g0:g0 + q]
            dtT_c = dtT_scr[r, :, g0:g0 + q]
            last = la_c[q - 1:q, :]
            e_la = _expand_heads(jnp.exp(la_c))
            w_in = _expand_heads(jnp.exp(last - la_c) * dt_c)
            e_last = _expand_heads(jnp.exp(last))
            xw_bf = (xs * w_in).astype(BF16)
            if q == SSM_HEAD_DIM:
                la_cols = _expand_heads(la_c)

            ys = []
            for g in range(SSM_GROUPS):
                gs = slice(g * GROUP_COLS, (g + 1) * GROUP_COLS)
                h0 = HEADS_PER_GROUP * g
                bg = b_bf[:, g * SSM_STATE:(g + 1) * SSM_STATE]
                cg = c_bf[:, g * SSM_STATE:(g + 1) * SSM_STATE]
                cb = lax.dot_general(cg, jnp.concatenate([bg] * HEADS_PER_GROUP, axis=0), NT_DIMS,
                                     preferred_element_type=F32)
                if q == SSM_HEAD_DIM:
                    la_col = la_cols[:, gs]
                else:
                    la_col = la_c[:, h0 + HEADS_PER_GROUP - 1:h0 + HEADS_PER_GROUP]
                    for hr in range(HEADS_PER_GROUP - 2, -1, -1):
                        la_col = jnp.where(lane_cat < (hr + 1) * q, la_c[:, h0 + hr:h0 + hr + 1], la_col)
                la_row = jnp.concatenate([laT_c[h0 + hr:h0 + hr + 1, :] for hr in range(HEADS_PER_GROUP)], axis=1)
                dt_row = jnp.concatenate([dtT_c[h0 + hr:h0 + hr + 1, :] for hr in range(HEADS_PER_GROUP)], axis=1)
                dec = jnp.exp(jnp.where(causal, la_col - la_row, NEG_BIG).astype(BF16))
                m_cat = cb.astype(BF16) * dec * dt_row.astype(BF16)
                xg = xs_bf[:, gs]
                x_bd = jnp.concatenate([jnp.where(head_masks[hr], xg, jnp.zeros_like(xg))
                                        for hr in range(HEADS_PER_GROUP)], axis=0)
                y_in = jnp.dot(m_cat, x_bd, preferred_element_type=F32)
                hT = h_scr[s, g]
                y_st = jnp.dot(cg, hT.astype(BF16), preferred_element_type=F32)
                ys.append(y_in + y_st * e_la[:, gs])
                upd = lax.dot_general(bg, xw_bf[:, gs], TN_DIMS, preferred_element_type=F32)
                h_scr[s, g] = hT * e_last[:, gs] + upd

            y = jnp.concatenate(ys, axis=1) + dexp_ref[...] * xs
            gated = y * _silu(z_scr[r, g0:g0 + q, :])
            parts = []
            for k in range(D_INNER // NORM_GROUP):
                gk = gated[:, k * NORM_GROUP:(k + 1) * NORM_GROUP]
                parts.append(gk * lax.rsqrt(jnp.mean(gk * gk, axis=-1, keepdims=True) + EPS))
            y_scr[g0:g0 + q, :] = (jnp.concatenate(parts, axis=1) * normw_ref[...]).astype(BF16)

        def mix_finish():
            for s in range(S):
                before_scr[s] = xbc_scr[r, s, T - q:T, :]
            y_nat = jnp.dot(permT_ref[...], y_scr[...], preferred_element_type=F32).astype(BF16)
            out = jnp.dot(y_nat, wout_ref[...], preferred_element_type=F32) + xres_scr[r]
            out_ref[...] = out.reshape(S, T, D_MODEL)

            @pl.when(last_of_stream)
            def _emit_state():
                for s in range(S):
                    for m, rho in enumerate(tail_rows):
                        convn_ref[s, m:m + 1, :] = before_scr[s, rho:rho + 1, :]
                    for g in range(SSM_GROUPS):
                        hg = h_scr[s, g].T.reshape(HEADS_PER_GROUP, SSM_HEAD_DIM, SSM_STATE)
                        ssmn_ref[s, HEADS_PER_GROUP * g:HEADS_PER_GROUP * (g + 1)] = hg

        proj_items = [functools.partial(proj_block, n) for n in range(ZXBC // PROJ_BLOCK)] + [dt_block]
        mix_items = [functools.partial(mix_chunk, s, c) for s in range(S) for c in range(n_chunks)]
        proj_items[0]()
        rest = np.array_split(np.arange(1, len(proj_items)), len(mix_items))
        for mix, idxs in zip(mix_items, rest):
            mix()
            for k in idxs:
                proj_items[k]()
        mix_finish()

    @pl.when(i % 2 == 0)
    def _even():
        step(0, 1)

    @pl.when(i % 2 == 1)
    def _odd():
        step(1, 0)


def _const_spec(shape):
    zeros = (0,) * len(shape)
    return pl.BlockSpec(shape, lambda *_: zeros, pipeline_mode=pl.Buffered(1))


def _row_order_matrices(M, q):
    rows = np.arange(M)
    chunk = rows // q
    tok_of_row = _chunk_token(rows, q)
    same = chunk[:, None] == chunk[None, :]
    perm = same & (tok_of_row[:, None] == (rows % q)[None, :])
    cums = same & (tok_of_row[:, None] >= (rows % q)[None, :])
    as_bf = lambda m: jnp.asarray(m, dtype=BF16)
    return as_bf(perm), as_bf(perm.T), as_bf(cums)


def _ssd_layer(x, conv0, ssm0, rms_w, w_zxbc, w_dt, conv_w, conv_b, dtb, alog, d_exp, norm_w, w_out, *, S, T, q):
    nb, seq, _ = x.shape
    assert nb % S == 0 and seq % T == 0 and T % q == 0 and q % SUBLANES == 0
    assert (conv0 is None) == (ssm0 is None)
    zero_state = conv0 is None
    M = S * T
    nt = seq // T
    n_tiles = (nb // S) * nt
    kern = functools.partial(_ssd_kernel, S=S, T=T, q=q, nt=nt, zero_state=zero_state)

    def proj_tile(i):
        ia = jnp.minimum(i, n_tiles - 1)
        return ia // nt, ia % nt

    def mix_tile(i):
        ib = jnp.maximum(i - 1, 0)
        return ib // nt, ib % nt

    in_specs = [
        pl.BlockSpec((S, T, D_MODEL), lambda i: (*proj_tile(i), 0)),
        _const_spec((1, D_MODEL)),
        _const_spec((D_MODEL, ZXBC)),
        _const_spec((D_MODEL, 2 * LANES)),
        _const_spec((M, M)),
        _const_spec((M, M)),
        _const_spec((M, M)),
        _const_spec((CONV_WIDTH, CONV_DIM)),
        _const_spec((1, CONV_DIM)),
        _const_spec((1, LANES)),
        _const_spec((1, LANES)),
        _const_spec((1, D_INNER)),
        _const_spec((1, D_INNER)),
        _const_spec((D_INNER, D_MODEL)),
    ]
    if not zero_state:
        in_specs += [
            pl.BlockSpec((S, CONV_WIDTH - 1, CONV_DIM), lambda i: (mix_tile(i)[0], 0, 0)),
            pl.BlockSpec((S, SSM_HEADS, SSM_HEAD_DIM, SSM_STATE), lambda i: (mix_tile(i)[0], 0, 0, 0)),
        ]
    out_specs = [
        pl.BlockSpec((S, T, D_MODEL), lambda i: (*mix_tile(i), 0)),
        pl.BlockSpec((S, CONV_WIDTH - 1, CONV_DIM), lambda i: (mix_tile(i)[0], 0, 0)),
        pl.BlockSpec((S, SSM_HEADS, SSM_HEAD_DIM, SSM_STATE), lambda i: (mix_tile(i)[0], 0, 0, 0)),
    ]
    out_shape = [
        jax.ShapeDtypeStruct((nb, seq, D_MODEL), F32),
        jax.ShapeDtypeStruct((nb, CONV_WIDTH - 1, CONV_DIM), F32),
        jax.ShapeDtypeStruct((nb, SSM_HEADS, SSM_HEAD_DIM, SSM_STATE), F32),
    ]
    scratch = [
        pltpu.VMEM((2, S, T, CONV_DIM), F32),
        pltpu.VMEM((S, q, CONV_DIM), F32),
        pltpu.VMEM((2, M, D_INNER), F32),
        pltpu.VMEM((2, M, D_MODEL), F32),
        pltpu.VMEM((2, M, LANES), F32),
        pltpu.VMEM((2, M, LANES), F32),
        pltpu.VMEM((2, LANES, M), F32),
        pltpu.VMEM((2, LANES, M), F32),
        pltpu.VMEM((M, D_INNER), BF16),
        pltpu.VMEM((S, SSM_GROUPS, SSM_STATE, GROUP_COLS), F32),
    ]
    return pl.pallas_call(
        kern,
        grid=(n_tiles + 1,),
        in_specs=in_specs,
        out_specs=out_specs,
        out_shape=out_shape,
        scratch_shapes=scratch,
        compiler_params=pltpu.CompilerParams(
            dimension_semantics=("arbitrary",),
            vmem_limit_bytes=VMEM_LIMIT_BYTES),
        name=f"ssd_layer_T{T}",
    )(x, rms_w, w_zxbc, w_dt, *_row_order_matrices(M, q), conv_w, conv_b, dtb, alog, d_exp, norm_w, w_out,
      *(() if zero_state else (conv0, ssm0)))


def _gmlp_kernel(x_ref, rmsw_ref, win_ref, lnw_ref, lnb_ref, wsp_ref, bexp_ref, wout_ref, fw_ref,
                 *refs, S, T, q, emit_v):
    if emit_v:
        out_ref, v_ref, proj_scr, y_scr = refs
    else:
        out_ref, proj_scr, y_scr = refs
        v_ref = None
    M = S * T
    x = x_ref[...].reshape(M, D_MODEL)
    hn_bf = _rms(x, rmsw_ref[...]).astype(BF16)
    for n in range(3 * G_WIDTH // PROJ_BLOCK):
        cols = slice(n * PROJ_BLOCK, (n + 1) * PROJ_BLOCK)
        proj_scr[:, cols] = jnp.dot(hn_bf, win_ref[:, cols], preferred_element_type=F32)

    row = lax.broadcasted_iota(jnp.int32, (q, q), 0)
    col = lax.broadcasted_iota(jnp.int32, (q, q), 1)
    block_causal = (col // CAUSAL_CHUNK) <= (row // CAUSAL_CHUNK)
    wms = [jnp.where(block_causal, wsp_ref[g, 0:q, 0:q], 0.0).astype(BF16) for g in range(G_GROUPS)]

    for s in range(S):
        for c in range(T // q):
            g0 = s * T + c * q
            u = _gelu(proj_scr[g0:g0 + q, 0:G_WIDTH])
            v = _gelu(proj_scr[g0:g0 + q, G_WIDTH:2 * G_WIDTH])
            vc = v - jnp.mean(v, axis=-1, keepdims=True)
            vn = vc * lax.rsqrt(jnp.mean(vc * vc, axis=-1, keepdims=True) + EPS) * lnw_ref[...] + lnb_ref[...]
            if emit_v:
                v_ref[s, c * q:(c + 1) * q, :] = vn
            vn_bf = vn.astype(BF16)
            parts = [jnp.dot(wms[g], vn_bf[:, g * G_GROUP_DIM:(g + 1) * G_GROUP_DIM],
                             preferred_element_type=F32) for g in range(G_GROUPS)]
            sp = jnp.concatenate(parts, axis=1) + bexp_ref[0:q, :]
            y = u * sp * _silu(proj_scr[g0:g0 + q, 2 * G_WIDTH:3 * G_WIDTH])
            y_scr[g0:g0 + q, :] = y.astype(BF16)

    out = jnp.dot(y_scr[...], wout_ref[...], preferred_element_type=F32) + x
    out_ref[...] = _rms(out, fw_ref[...]).reshape(S, T, D_MODEL)


def _gmlp_layer(x, rms_w, w_in, ln_w, ln_b, w_sp, b_exp, w_out, final_w, *, S, T, q, emit_v):
    nb, seq, _ = x.shape
    assert nb % S == 0 and seq % T == 0 and T % q == 0
    M = S * T
    kern = functools.partial(_gmlp_kernel, S=S, T=T, q=q, emit_v=emit_v)
    in_specs = [
        pl.BlockSpec((S, T, D_MODEL), lambda b, t: (b, t, 0)),
        _const_spec((1, D_MODEL)),
        _const_spec((D_MODEL, 3 * G_WIDTH)),
        _const_spec((1, G_WIDTH)),
        _const_spec((1, G_WIDTH)),
        _const_spec((G_GROUPS, G_CHUNK, G_CHUNK)),
        _const_spec((G_CHUNK, G_WIDTH)),
        _const_spec((G_WIDTH, D_MODEL)),
        _const_spec((1, D_MODEL)),
    ]
    out_specs = [pl.BlockSpec((S, T, D_MODEL), lambda b, t: (b, t, 0))]
    out_shape = [jax.ShapeDtypeStruct((nb, seq, D_MODEL), F32)]
    if emit_v:
        out_specs.append(pl.BlockSpec((S, T, G_WIDTH), lambda b, t: (b, t, 0)))
        out_shape.append(jax.ShapeDtypeStruct((nb, seq, G_WIDTH), F32))
    scratch = [pltpu.VMEM((M, 3 * G_WIDTH), F32), pltpu.VMEM((M, G_WIDTH), BF16)]
    return pl.pallas_call(
        kern,
        grid=(nb // S, seq // T),
        in_specs=in_specs,
        out_specs=out_specs,
        out_shape=out_shape,
        scratch_shapes=scratch,
        compiler_params=pltpu.CompilerParams(
            dimension_semantics=("arbitrary", "arbitrary"),
            vmem_limit_bytes=VMEM_LIMIT_BYTES),
        name=f"gmlp_layer_T{T}",
    )(x, rms_w, w_in, ln_w, ln_b, w_sp, b_exp, w_out, final_w)


def _tiling(seq, chunk, max_tile):
    return min(seq, max_tile), min(chunk, seq)


def kernel(x_prompt, x_sample, cache_conv, state_ssm, norm_w, final_norm_w, a_w_in, a_conv_w, a_conv_b,
           a_dt_bias, a_log, a_d, a_norm_w, a_w_out, b_w_in, b_ln_w, b_ln_b, b_w_sp, b_b_sp, b_w_out):
    assert a_w_in.shape[0] == 1 and b_w_in.shape[0] == 1 and norm_w.shape[0] == 2

    w_in = a_w_in.reshape(D_MODEL, ZXBC + SSM_HEADS)
    w_dt = jnp.pad(w_in[:, ZXBC:], ((0, 0), (0, LANES - SSM_HEADS)))
    w_dt_hi = w_dt.astype(BF16)
    w_dt_lo = (w_dt - w_dt_hi.astype(F32)).astype(BF16)
    pad_heads = lambda v: jnp.pad(v.reshape(1, SSM_HEADS), ((0, 0), (0, LANES - SSM_HEADS)))
    ssd_params = dict(
        rms_w=norm_w[0].reshape(1, D_MODEL),
        w_zxbc=a_w_in.astype(BF16).reshape(D_MODEL, ZXBC + SSM_HEADS),
        w_dt=jnp.concatenate([w_dt_hi, w_dt_lo], axis=1),
        conv_w=a_conv_w.reshape(CONV_WIDTH, CONV_DIM), conv_b=a_conv_b.reshape(1, CONV_DIM),
        dtb=pad_heads(a_dt_bias), alog=pad_heads(a_log),
        d_exp=jnp.repeat(a_d.reshape(SSM_HEADS), SSM_HEAD_DIM).reshape(1, D_INNER),
        norm_w=a_norm_w.reshape(1, D_INNER), w_out=a_w_out.reshape(D_INNER, D_MODEL).astype(BF16))
    gmlp_params = dict(
        rms_w=norm_w[1].reshape(1, D_MODEL), w_in=b_w_in.reshape(D_MODEL, 3 * G_WIDTH).astype(BF16),
        ln_w=b_ln_w.reshape(1, G_WIDTH), ln_b=b_ln_b.reshape(1, G_WIDTH),
        w_sp=b_w_sp.reshape(G_GROUPS, G_CHUNK, G_CHUNK),
        b_exp=jnp.repeat(b_b_sp.reshape(G_GROUPS, G_CHUNK).T, G_GROUP_DIM, axis=1),
        w_out=b_w_out.reshape(G_WIDTH, D_MODEL).astype(BF16), final_w=final_norm_w.reshape(1, D_MODEL))

    bp, lp, _ = x_prompt.shape
    bs, ls, _ = x_sample.shape

    Tp, qp = _tiling(lp, CAUSAL_CHUNK, SSD_TILE)
    hp, conv_p, ssm_p = _ssd_layer(x_prompt, None, None, **ssd_params, S=1, T=Tp, q=qp)
    Tg, qg = _tiling(lp, G_CHUNK, GMLP_TILE)
    (y_prompt,) = _gmlp_layer(hp, **gmlp_params, S=1, T=Tg, q=qg, emit_v=False)

    Ts, qs = _tiling(ls, CAUSAL_CHUNK, SSD_TILE)
    hs, conv_s, ssm_s = _ssd_layer(x_sample, cache_conv.reshape(bs, CONV_WIDTH - 1, CONV_DIM),
                                   state_ssm.reshape(bs, SSM_HEADS, SSM_HEAD_DIM, SSM_STATE),
                                   **ssd_params, S=math.gcd(bs, SSD_SAMPLE_STREAMS), T=Ts, q=qs)
    Tg, qg = _tiling(ls, G_CHUNK, GMLP_TILE)
    y_sample, v_s = _gmlp_layer(hs, **gmlp_params, S=math.gcd(bs, GMLP_SAMPLE_STREAMS), T=Tg, q=qg, emit_v=True)

    lead = lambda a: a.reshape((1,) + a.shape)
    return (y_prompt, y_sample, lead(conv_p), lead(ssm_p), lead(conv_s), lead(ssm_s), lead(v_s))
```

```python
import functools
import math

import numpy as np
import jax
import jax.numpy as jnp
from jax import lax
from jax.experimental import pallas as pl
from jax.experimental.pallas import tpu as pltpu

F32 = jnp.float32
BF16 = jnp.bfloat16

EPS = 1e-5
D_MODEL = 1024
D_INNER = 2048
SSM_HEADS = 32
SSM_HEAD_DIM = 64
SSM_STATE = 128
SSM_GROUPS = 8
HEADS_PER_GROUP = 4
GROUP_COLS = HEADS_PER_GROUP * SSM_HEAD_DIM
CONV_WIDTH = 4
GN = SSM_GROUPS * SSM_STATE
CONV_DIM = D_INNER + 2 * GN
NORM_GROUP = 256
G_WIDTH = 2048
G_GROUPS = 8
G_GROUP_DIM = 256
G_CHUNK = 128
CAUSAL_CHUNK = 64
ZXBC = D_INNER + CONV_DIM
PROJ_BLOCK = 1024
SSD_TILE = 256
GMLP_TILE = 512
SSD_SAMPLE_STREAMS = 4
GMLP_SAMPLE_STREAMS = 32
LANES = 128
SUBLANES = 8
VMEM_LIMIT_BYTES = 58 * 1024 * 1024
NEG_BIG = -1e30

NT_DIMS = (((1,), (1,)), ((), ()))
TN_DIMS = (((0,), (0,)), ((), ()))


def _silu(x):
    return x / (1.0 + jnp.exp(-x))


def _softplus(x):
    return jnp.maximum(x, 0.0) + jnp.log1p(jnp.exp(-jnp.abs(x)))


def _gelu(x):
    return 0.5 * x * (1.0 + lax.erf(x * 0.7071067811865476))


def _rms(x, w):
    return x * lax.rsqrt(jnp.mean(x * x, axis=-1, keepdims=True) + EPS) * w


def _split3(v):
    hi = v.astype(BF16)
    r1 = v - hi.astype(F32)
    mid = r1.astype(BF16)
    lo = (r1 - mid.astype(F32)).astype(BF16)
    return hi, mid, lo


def _sum3(p):
    return p[:, :LANES] + (p[:, LANES:2 * LANES] + p[:, 2 * LANES:])


def _expand_heads(v):
    lane = lax.broadcasted_iota(jnp.int32, (v.shape[0], LANES), 1)
    first = lane < SSM_HEAD_DIM
    pieces = [jnp.where(first, v[:, 2 * j:2 * j + 1], v[:, 2 * j + 1:2 * j + 2])
              for j in range(SSM_HEADS // 2)]
    return jnp.concatenate(pieces, axis=1)


def _chunk_token(row, q):
    r = row % q
    return r // SUBLANES + (q // SUBLANES) * (r % SUBLANES)


def _token_row(tau, q):
    nv = q // SUBLANES
    return SUBLANES * (tau % nv) + tau // nv


def _ssd_kernel(x_ref, rmsw_ref, win_ref, wdt_ref, perm_ref, permT_ref, cums_ref, convw_ref, convb_ref,
                dtb_ref, alog_ref, dexp_ref, normw_ref, wout_ref, *refs, S, T, q, nt, zero_state):
    if zero_state:
        conv0_ref = ssm0_ref = None
    else:
        conv0_ref, ssm0_ref, *refs = refs
    (out_ref, convn_ref, ssmn_ref,
     xbc_scr, before_scr, z_scr, xres_scr, la_scr, dt_scr, laT_scr, dtT_scr, y_scr, h_scr) = refs
    i = pl.program_id(0)
    M = S * T
    nv = q // SUBLANES
    n_chunks = T // q
    tail_rows = [_token_row(q - (CONV_WIDTH - 1) + m, q) for m in range(CONV_WIDTH - 1)]
    mix_tile = jnp.maximum(i - 1, 0) % nt
    first_of_stream = (i >= 1) & (mix_tile == 0)
    last_of_stream = (i >= 1) & (mix_tile == nt - 1)

    @pl.when(i == 0)
    def _prime():
        xbc_scr[1] = jnp.zeros((S, T, CONV_DIM), F32)
        z_scr[1] = jnp.zeros((M, D_INNER), F32)
        xres_scr[1] = jnp.zeros((M, D_MODEL), F32)
        la_scr[1] = jnp.zeros((M, LANES), F32)
        dt_scr[1] = jnp.zeros((M, LANES), F32)
        laT_scr[1] = jnp.zeros((LANES, M), F32)
        dtT_scr[1] = jnp.zeros((LANES, M), F32)
        before_scr[...] = jnp.zeros((S, q, CONV_DIM), F32)
        h_scr[...] = jnp.zeros((S, SSM_GROUPS, SSM_STATE, GROUP_COLS), F32)

    @pl.when(first_of_stream)
    def _load_state():
        for s in range(S):
            before_scr[s] = jnp.zeros((q, CONV_DIM), F32)
            if zero_state:
                h_scr[s] = jnp.zeros((SSM_GROUPS, SSM_STATE, GROUP_COLS), F32)
                continue
            for m, rho in enumerate(tail_rows):
                before_scr[s, rho:rho + 1, :] = conv0_ref[s, m:m + 1, :]
            for g in range(SSM_GROUPS):
                hg = ssm0_ref[s, HEADS_PER_GROUP * g:HEADS_PER_GROUP * (g + 1)]
                h_scr[s, g] = hg.reshape(GROUP_COLS, SSM_STATE).T

    def step(w, r):
        x = x_ref[...].reshape(M, D_MODEL)
        xres_scr[w] = x
        hn = _rms(x, rmsw_ref[...])
        hn_bf = hn.astype(BF16)
        hn_p = jnp.dot(perm_ref[...], hn_bf, preferred_element_type=F32).astype(BF16)

        def proj_block(n):
            blk = jnp.dot(hn_p, win_ref[:, n * PROJ_BLOCK:(n + 1) * PROJ_BLOCK], preferred_element_type=F32)
            c0 = n * PROJ_BLOCK
            if c0 < D_INNER:
                z_scr[w, :, c0:c0 + PROJ_BLOCK] = blk
            else:
                for s in range(S):
                    xbc_scr[w, s, :, c0 - D_INNER:c0 - D_INNER + PROJ_BLOCK] = blk[s * T:(s + 1) * T]

        def dt_block():
            dt_raw = jnp.dot(hn_p, wdt_ref[...], preferred_element_type=F32)
            dt = _softplus(dt_raw + dtb_ref[...])
            dta = dt * (-jnp.exp(alog_ref[...]))
            la = _sum3(jnp.dot(cums_ref[...], jnp.concatenate(_split3(dta), axis=1), preferred_element_type=F32))
            la_scr[w] = la
            dt_scr[w] = dt
            laT_scr[w] = la.T
            dtT_scr[w] = dt.T

        tok_t = _chunk_token(lax.broadcasted_iota(jnp.int32, (q, HEADS_PER_GROUP * q), 0), q)
        tok_s = _chunk_token(lax.broadcasted_iota(jnp.int32, (q, HEADS_PER_GROUP * q), 1), q)
        causal = tok_t >= tok_s
        lane_cat = lax.broadcasted_iota(jnp.int32, (q, HEADS_PER_GROUP * q), 1)
        lane_g = lax.broadcasted_iota(jnp.int32, (q, GROUP_COLS), 1)
        head_masks = [(lane_g >= SSM_HEAD_DIM * hr) & (lane_g < SSM_HEAD_DIM * (hr + 1))
                      for hr in range(HEADS_PER_GROUP)]
        sub_iota = lax.broadcasted_iota(jnp.int32, (SUBLANES, CONV_DIM), 0)

        def mix_chunk(s, c):
            base = c * q
            g0 = s * T + c * q

            def halo(v, sh):
                cur = xbc_scr[r, s, base + SUBLANES * v:base + SUBLANES * (v + 1), :]
                if c == 0:
                    prv = before_scr[s, SUBLANES * v:SUBLANES * (v + 1), :]
                else:
                    prv = xbc_scr[r, s, base - q + SUBLANES * v:base - q + SUBLANES * (v + 1), :]
                return jnp.where(sub_iota >= sh, pltpu.roll(cur, sh, 0), pltpu.roll(prv, sh, 0))

            acc = convb_ref[...] + convw_ref[CONV_WIDTH - 1:CONV_WIDTH, :] * xbc_scr[r, s, base:base + q, :]
            for d in range(1, CONV_WIDTH):
                pieces = []
                for j in range(min(d, nv)):
                    jj = j - d
                    pieces.append(halo(jj % nv, -(jj // nv)))
                if q - SUBLANES * d > 0:
                    pieces.append(xbc_scr[r, s, base:base + q - SUBLANES * d, :])
                src = pieces[0] if len(pieces) == 1 else jnp.concatenate(pieces, axis=0)
                acc = acc + convw_ref[CONV_WIDTH - 1 - d:CONV_WIDTH - d, :] * src
            xc = _silu(acc.astype(BF16))
            xs_bf = xc[:, :D_INNER]
            xs = xs_bf.astype(F32)
            b_bf = xc[:, D_INNER:D_INNER + GN]
            c_bf = xc[:, D_INNER + GN:]

            la_c = la_scr[r, g0:g0 + q, :]
            dt_c = dt_scr[r, g0:g0 + q, :]
            laT_c = laT_scr[r, :, g0:g0 + q]
            dtT_c = dtT_scr[r, :, g0:g0 + q]
            last = la_c[q - 1:q, :]
            e_la = _expand_heads(jnp.exp(la_c))
            w_in = _expand_heads(jnp.exp(last - la_c) * dt_c)
            e_last = _expand_heads(jnp.exp(last))
            xw_bf = (xs * w_in).astype(BF16)
            if q == SSM_HEAD_DIM:
                la_cols = _expand_heads(la_c)

            ys = []
            for g in range(SSM_GROUPS):
                gs = slice(g * GROUP_COLS, (g + 1) * GROUP_COLS)
                h0 = HEADS_PER_GROUP * g
                bg = b_bf[:, g * SSM_STATE:(g + 1) * SSM_STATE]
                cg = c_bf[:, g * SSM_STATE:(g + 1) * SSM_STATE]
                cb = lax.dot_general(cg, jnp.concatenate([bg] * HEADS_PER_GROUP, axis=0), NT_DIMS,
                                     preferred_element_type=F32)
                if q == SSM_HEAD_DIM:
                    la_col = la_cols[:, gs]
                else:
                    la_col = la_c[:, h0 + HEADS_PER_GROUP - 1:h0 + HEADS_PER_GROUP]
                    for hr in range(HEADS_PER_GROUP - 2, -1, -1):
                        la_col = jnp.where(lane_cat < (hr + 1) * q, la_c[:, h0 + hr:h0 + hr + 1], la_col)
                la_row = jnp.concatenate([laT_c[h0 + hr:h0 + hr + 1, :] for hr in range(HEADS_PER_GROUP)], axis=1)
                dt_row = jnp.concatenate([dtT_c[h0 + hr:h0 + hr + 1, :] for hr in range(HEADS_PER_GROUP)], axis=1)
                dec = jnp.exp(jnp.where(causal, la_col - la_row, NEG_BIG).astype(BF16))
                m_cat = cb.astype(BF16) * dec * dt_row.astype(BF16)
                xg = xs_bf[:, gs]
                x_bd = jnp.concatenate([jnp.where(head_masks[hr], xg, jnp.zeros_like(xg))
                                        for hr in range(HEADS_PER_GROUP)], axis=0)
                y_in = jnp.dot(m_cat, x_bd, preferred_element_type=F32)
                hT = h_scr[s, g]
                y_st = jnp.dot(cg, hT.astype(BF16), preferred_element_type=F32)
                ys.append(y_in + y_st * e_la[:, gs])
                upd = lax.dot_general(bg, xw_bf[:, gs], TN_DIMS, preferred_element_type=F32)
                h_scr[s, g] = hT * e_last[:, gs] + upd

            y = jnp.concatenate(ys, axis=1) + dexp_ref[...] * xs
            gated = y * _silu(z_scr[r, g0:g0 + q, :])
            parts = []
            for k in range(D_INNER // NORM_GROUP):
                gk = gated[:, k * NORM_GROUP:(k + 1) * NORM_GROUP]
                parts.append(gk * lax.rsqrt(jnp.mean(gk * gk, axis=-1, keepdims=True) + EPS))
            y_scr[g0:g0 + q, :] = (jnp.concatenate(parts, axis=1) * normw_ref[...]).astype(BF16)

        def mix_finish():
            for s in range(S):
                before_scr[s] = xbc_scr[r, s, T - q:T, :]
            y_nat = jnp.dot(permT_ref[...], y_scr[...], preferred_element_type=F32).astype(BF16)
            out = jnp.dot(y_nat, wout_ref[...], preferred_element_type=F32) + xres_scr[r]
            out_ref[...] = out.reshape(S, T, D_MODEL)

            @pl.when(last_of_stream)
            def _emit_state():
                for s in range(S):
                    for m, rho in enumerate(tail_rows):
                        convn_ref[s, m:m + 1, :] = before_scr[s, rho:rho + 1, :]
                    for g in range(SSM_GROUPS):
                        hg = h_scr[s, g].T.reshape(HEADS_PER_GROUP, SSM_HEAD_DIM, SSM_STATE)
                        ssmn_ref[s, HEADS_PER_GROUP * g:HEADS_PER_GROUP * (g + 1)] = hg

        proj_items = [functools.partial(proj_block, n) for n in range(ZXBC // PROJ_BLOCK)] + [dt_block]
        mix_items = [functools.partial(mix_chunk, s, c) for s in range(S) for c in range(n_chunks)]
        proj_items[0]()
        rest = np.array_split(np.arange(1, len(proj_items)), len(mix_items))
        for mix, idxs in zip(mix_items, rest):
            mix()
            for k in idxs:
                proj_items[k]()
        mix_finish()

    @pl.when(i % 2 == 0)
    def _even():
        step(0, 1)

    @pl.when(i % 2 == 1)
    def _odd():
        step(1, 0)


def _const_spec(shape):
    zeros = (0,) * len(shape)
    return pl.BlockSpec(shape, lambda *_: zeros, pipeline_mode=pl.Buffered(1))


def _row_order_matrices(M, q):
    rows = np.arange(M)
    chunk = rows // q
    tok_of_row = _chunk_token(rows, q)
    same = chunk[:, None] == chunk[None, :]
    perm = same & (tok_of_row[:, None] == (rows % q)[None, :])
    cums = same & (tok_of_row[:, None] >= tok_of_row[None, :])
    as_bf = lambda m: jnp.asarray(m, dtype=BF16)
    return as_bf(perm), as_bf(perm.T), as_bf(cums)


def _ssd_layer(x, conv0, ssm0, rms_w, w_zxbc, w_dt, conv_w, conv_b, dtb, alog, d_exp, norm_w, w_out, *, S, T, q):
    nb, seq, _ = x.shape
    assert nb % S == 0 and seq % T == 0 and T % q == 0 and q % SUBLANES == 0
    assert (conv0 is None) == (ssm0 is None)
    zero_state = conv0 is None
    M = S * T
    nt = seq // T
    n_tiles = (nb // S) * nt
    kern = functools.partial(_ssd_kernel, S=S, T=T, q=q, nt=nt, zero_state=zero_state)

    def proj_tile(i):
        ia = jnp.minimum(i, n_tiles - 1)
        return ia // nt, ia % nt

    def mix_tile(i):
        ib = jnp.maximum(i - 1, 0)
        return ib // nt, ib % nt

    in_specs = [
        pl.BlockSpec((S, T, D_MODEL), lambda i: (*proj_tile(i), 0)),
        _const_spec((1, D_MODEL)),
        _const_spec((D_MODEL, ZXBC)),
        _const_spec((D_MODEL, LANES)),
        _const_spec((M, M)),
        _const_spec((M, M)),
        _const_spec((M, M)),
        _const_spec((CONV_WIDTH, CONV_DIM)),
        _const_spec((1, CONV_DIM)),
        _const_spec((1, LANES)),
        _const_spec((1, LANES)),
        _const_spec((1, D_INNER)),
        _const_spec((1, D_INNER)),
        _const_spec((D_INNER, D_MODEL)),
    ]
    if not zero_state:
        in_specs += [
            pl.BlockSpec((S, CONV_WIDTH - 1, CONV_DIM), lambda i: (mix_tile(i)[0], 0, 0)),
            pl.BlockSpec((S, SSM_HEADS, SSM_HEAD_DIM, SSM_STATE), lambda i: (mix_tile(i)[0], 0, 0, 0)),
        ]
    out_specs = [
        pl.BlockSpec((S, T, D_MODEL), lambda i: (*mix_tile(i), 0)),
        pl.BlockSpec((S, CONV_WIDTH - 1, CONV_DIM), lambda i: (mix_tile(i)[0], 0, 0)),
        pl.BlockSpec((S, SSM_HEADS, SSM_HEAD_DIM, SSM_STATE), lambda i: (mix_tile(i)[0], 0, 0, 0)),
    ]
    out_shape = [
        jax.ShapeDtypeStruct((nb, seq, D_MODEL), F32),
        jax.ShapeDtypeStruct((nb, CONV_WIDTH - 1, CONV_DIM), F32),
        jax.ShapeDtypeStruct((nb, SSM_HEADS, SSM_HEAD_DIM, SSM_STATE), F32),
    ]
    scratch = [
        pltpu.VMEM((2, S, T, CONV_DIM), F32),
        pltpu.VMEM((S, q, CONV_DIM), F32),
        pltpu.VMEM((2, M, D_INNER), F32),
        pltpu.VMEM((2, M, D_MODEL), F32),
        pltpu.VMEM((2, M, LANES), F32),
        pltpu.VMEM((2, M, LANES), F32),
        pltpu.VMEM((2, LANES, M), F32),
        pltpu.VMEM((2, LANES, M), F32),
        pltpu.VMEM((M, D_INNER), BF16),
        pltpu.VMEM((S, SSM_GROUPS, SSM_STATE, GROUP_COLS), F32),
    ]
    return pl.pallas_call(
        kern,
        grid=(n_tiles + 1,),
        in_specs=in_specs,
        out_specs=out_specs,
        out_shape=out_shape,
        scratch_shapes=scratch,
        compiler_params=pltpu.CompilerParams(
            dimension_semantics=("arbitrary",),
            vmem_limit_bytes=VMEM_LIMIT_BYTES),
        name=f"ssd_layer_T{T}",
    )(x, rms_w, w_zxbc, w_dt, *_row_order_matrices(M, q), conv_w, conv_b, dtb, alog, d_exp, norm_w, w_out,
      *(() if zero_state else (conv0, ssm0)))


def _gmlp_kernel(x_ref, rmsw_ref, win_ref, lnw_ref, lnb_ref, wsp_ref, bexp_ref, wout_ref, fw_ref,
                 *refs, S, T, q, emit_v):
    if emit_v:
        out_ref, v_ref, proj_scr, y_scr = refs
    else:
        out_ref, proj_scr, y_scr = refs
        v_ref = None
    M = S * T
    x = x_ref[...].reshape(M, D_MODEL)
    hn_bf = _rms(x, rmsw_ref[...]).astype(BF16)
    for n in range(3 * G_WIDTH // PROJ_BLOCK):
        cols = slice(n * PROJ_BLOCK, (n + 1) * PROJ_BLOCK)
        proj_scr[:, cols] = jnp.dot(hn_bf, win_ref[:, cols], preferred_element_type=F32)

    row = lax.broadcasted_iota(jnp.int32, (q, q), 0)
    col = lax.broadcasted_iota(jnp.int32, (q, q), 1)
    block_causal = (col // CAUSAL_CHUNK) <= (row // CAUSAL_CHUNK)
    wms = [jnp.where(block_causal, wsp_ref[g, 0:q, 0:q], 0.0).astype(BF16) for g in range(G_GROUPS)]

    for s in range(S):
        for c in range(T // q):
            g0 = s * T + c * q
            u = _gelu(proj_scr[g0:g0 + q, 0:G_WIDTH])
            v = _gelu(proj_scr[g0:g0 + q, G_WIDTH:2 * G_WIDTH])
            vc = v - jnp.mean(v, axis=-1, keepdims=True)
            vn = vc * lax.rsqrt(jnp.mean(vc * vc, axis=-1, keepdims=True) + EPS) * lnw_ref[...] + lnb_ref[...]
            if emit_v:
                v_ref[s, c * q:(c + 1) * q, :] = vn
            vn_bf = vn.astype(BF16)
            parts = [jnp.dot(wms[g], vn_bf[:, g * G_GROUP_DIM:(g + 1) * G_GROUP_DIM],
                             preferred_element_type=F32) for g in range(G_GROUPS)]
            sp = jnp.concatenate(parts, axis=1) + bexp_ref[0:q, :]
            y = u * sp * _silu(proj_scr[g0:g0 + q, 2 * G_WIDTH:3 * G_WIDTH])
            y_scr[g0:g0 + q, :] = y.astype(BF16)

    out = jnp.dot(y_scr[...], wout_ref[...], preferred_element_type=F32) + x
    out_ref[...] = _rms(out, fw_ref[...]).reshape(S, T, D_MODEL)


def _gmlp_layer(x, rms_w, w_in, ln_w, ln_b, w_sp, b_exp, w_out, final_w, *, S, T, q, emit_v):
    nb, seq, _ = x.shape
    assert nb % S == 0 and seq % T == 0 and T % q == 0
    M = S * T
    kern = functools.partial(_gmlp_kernel, S=S, T=T, q=q, emit_v=emit_v)
    in_specs = [
        pl.BlockSpec((S, T, D_MODEL), lambda b, t: (b, t, 0)),
        _const_spec((1, D_MODEL)),
        _const_spec((D_MODEL, 3 * G_WIDTH)),
        _const_spec((1, G_WIDTH)),
        _const_spec((1, G_WIDTH)),
        _const_spec((G_GROUPS, G_CHUNK, G_CHUNK)),
        _const_spec((G_CHUNK, G_WIDTH)),
        _const_spec((G_WIDTH, D_MODEL)),
        _const_spec((1, D_MODEL)),
    ]
    out_specs = [pl.BlockSpec((S, T, D_MODEL), lambda b, t: (b, t, 0))]
    out_shape = [jax.ShapeDtypeStruct((nb, seq, D_MODEL), F32)]
    if emit_v:
        out_specs.append(pl.BlockSpec((S, T, G_WIDTH), lambda b, t: (b, t, 0)))
        out_shape.append(jax.ShapeDtypeStruct((nb, seq, G_WIDTH), F32))
    scratch = [pltpu.VMEM((M, 3 * G_WIDTH), F32), pltpu.VMEM((M, G_WIDTH), BF16)]
    return pl.pallas_call(
        kern,
        grid=(nb // S, seq // T),
        in_specs=in_specs,
        out_specs=out_specs,
        out_shape=out_shape,
        scratch_shapes=scratch,
        compiler_params=pltpu.CompilerParams(
            dimension_semantics=("arbitrary", "arbitrary"),
            vmem_limit_bytes=VMEM_LIMIT_BYTES),
        name=f"gmlp_layer_T{T}",
    )(x, rms_w, w_in, ln_w, ln_b, w_sp, b_exp, w_out, final_w)


def _tiling(seq, chunk, max_tile):
    return min(seq, max_tile), min(chunk, seq)


def kernel(x_prompt, x_sample, cache_conv, state_ssm, norm_w, final_norm_w, a_w_in, a_conv_w, a_conv_b,
           a_dt_bias, a_log, a_d, a_norm_w, a_w_out, b_w_in, b_ln_w, b_ln_b, b_w_sp, b_b_sp, b_w_out):
    assert a_w_in.shape[0] == 1 and b_w_in.shape[0] == 1 and norm_w.shape[0] == 2

    w_in_bf = a_w_in.astype(BF16).reshape(D_MODEL, ZXBC + SSM_HEADS)
    pad_heads = lambda v: jnp.pad(v.reshape(1, SSM_HEADS), ((0, 0), (0, LANES - SSM_HEADS)))
    ssd_params = dict(
        rms_w=norm_w[0].reshape(1, D_MODEL),
        w_zxbc=w_in_bf,
        w_dt=jnp.pad(w_in_bf[:, ZXBC:], ((0, 0), (0, LANES - SSM_HEADS))),
        conv_w=a_conv_w.reshape(CONV_WIDTH, CONV_DIM), conv_b=a_conv_b.reshape(1, CONV_DIM),
        dtb=pad_heads(a_dt_bias), alog=pad_heads(a_log),
        d_exp=jnp.repeat(a_d.reshape(SSM_HEADS), SSM_HEAD_DIM).reshape(1, D_INNER),
        norm_w=a_norm_w.reshape(1, D_INNER), w_out=a_w_out.reshape(D_INNER, D_MODEL).astype(BF16))
    gmlp_params = dict(
        rms_w=norm_w[1].reshape(1, D_MODEL), w_in=b_w_in.reshape(D_MODEL, 3 * G_WIDTH).astype(BF16),
        ln_w=b_ln_w.reshape(1, G_WIDTH), ln_b=b_ln_b.reshape(1, G_WIDTH),
        w_sp=b_w_sp.reshape(G_GROUPS, G_CHUNK, G_CHUNK),
        b_exp=jnp.repeat(b_b_sp.reshape(G_GROUPS, G_CHUNK).T, G_GROUP_DIM, axis=1),
        w_out=b_w_out.reshape(G_WIDTH, D_MODEL).astype(BF16), final_w=final_norm_w.reshape(1, D_MODEL))

    bp, lp, _ = x_prompt.shape
    bs, ls, _ = x_sample.shape

    Tp, qp = _tiling(lp, CAUSAL_CHUNK, SSD_TILE)
    hp, conv_p, ssm_p = _ssd_layer(x_prompt, None, None, **ssd_params, S=1, T=Tp, q=qp)
    Tg, qg = _tiling(lp, G_CHUNK, GMLP_TILE)
    (y_prompt,) = _gmlp_layer(hp, **gmlp_params, S=1, T=Tg, q=qg, emit_v=False)

    Ts, qs = _tiling(ls, CAUSAL_CHUNK, SSD_TILE)
    hs, conv_s, ssm_s = _ssd_layer(x_sample, cache_conv.reshape(bs, CONV_WIDTH - 1, CONV_DIM),
                                   state_ssm.reshape(bs, SSM_HEADS, SSM_HEAD_DIM, SSM_STATE),
                                   **ssd_params, S=math.gcd(bs, SSD_SAMPLE_STREAMS), T=Ts, q=qs)
    Tg, qg = _tiling(ls, G_CHUNK, GMLP_TILE)
    y_sample, v_s = _gmlp_layer(hs, **gmlp_params, S=math.gcd(bs, GMLP_SAMPLE_STREAMS), T=Tg, q=qg, emit_v=True)

    lead = lambda a: a.reshape((1,) + a.shape)
    return (y_prompt, y_sample, lead(conv_p), lead(ssm_p), lead(conv_s), lead(ssm_s), lead(v_s))
```

```python
import functools
import math

import numpy as np
import jax
import jax.numpy as jnp
from jax import lax
from jax.experimental import pallas as pl
from jax.experimental.pallas import tpu as pltpu

F32 = jnp.float32
BF16 = jnp.bfloat16

EPS = 1e-5
D_MODEL = 1024
D_INNER = 2048
SSM_HEADS = 32
SSM_HEAD_DIM = 64
SSM_STATE = 128
SSM_GROUPS = 8
HEADS_PER_GROUP = 4
GROUP_COLS = HEADS_PER_GROUP * SSM_HEAD_DIM
CONV_WIDTH = 4
GN = SSM_GROUPS * SSM_STATE
CONV_DIM = D_INNER + 2 * GN
NORM_GROUP = 256
G_WIDTH = 2048
G_GROUPS = 8
G_GROUP_DIM = 256
G_CHUNK = 128
CAUSAL_CHUNK = 64
ZXBC = D_INNER + CONV_DIM
PROJ_BLOCK = 1024
SSD_TILE = 256
GMLP_TILE = 512
SSD_SAMPLE_STREAMS = 4
GMLP_SAMPLE_STREAMS = 32
LANES = 128
SUBLANES = 8
VMEM_LIMIT_BYTES = 58 * 1024 * 1024
NEG_BIG = -1e30

NT_DIMS = (((1,), (1,)), ((), ()))
TN_DIMS = (((0,), (0,)), ((), ()))


def _silu(x):
    return x / (1.0 + jnp.exp(-x))


def _softplus(x):
    return jnp.maximum(x, 0.0) + jnp.log1p(jnp.exp(-jnp.abs(x)))


def _gelu(x):
    return 0.5 * x * (1.0 + lax.erf(x * 0.7071067811865476))


def _rms(x, w):
    return x * lax.rsqrt(jnp.mean(x * x, axis=-1, keepdims=True) + EPS) * w


def _split3(v):
    hi = v.astype(BF16)
    r1 = v - hi.astype(F32)
    mid = r1.astype(BF16)
    lo = (r1 - mid.astype(F32)).astype(BF16)
    return hi, mid, lo


def _sum3(p):
    return p[:, :LANES] + (p[:, LANES:2 * LANES] + p[:, 2 * LANES:])


def _expand_heads(v):
    lane = lax.broadcasted_iota(jnp.int32, (v.shape[0], LANES), 1)
    first = lane < SSM_HEAD_DIM
    pieces = [jnp.where(first, v[:, 2 * j:2 * j + 1], v[:, 2 * j + 1:2 * j + 2])
              for j in range(SSM_HEADS // 2)]
    return jnp.concatenate(pieces, axis=1)


def _chunk_token(row, q):
    r = row % q
    return r // SUBLANES + (q // SUBLANES) * (r % SUBLANES)


def _token_row(tau, q):
    nv = q // SUBLANES
    return SUBLANES * (tau % nv) + tau // nv


def _ssd_kernel(x_ref, rmsw_ref, win_ref, wdt_ref, perm_ref, permT_ref, cums_ref, convw_ref, convb_ref,
                dtb_ref, alog_ref, dexp_ref, normw_ref, wout_ref, *refs, S, T, q, nt, zero_state):
    if zero_state:
        conv0_ref = ssm0_ref = None
    else:
        conv0_ref, ssm0_ref, *refs = refs
    (out_ref, convn_ref, ssmn_ref,
     xbc_scr, before_scr, z_scr, xres_scr, la_scr, dt_scr, laT_scr, dtT_scr, y_scr, h_scr) = refs
    i = pl.program_id(0)
    M = S * T
    nv = q // SUBLANES
    n_chunks = T // q
    tail_rows = [_token_row(q - (CONV_WIDTH - 1) + m, q) for m in range(CONV_WIDTH - 1)]
    mix_tile = jnp.maximum(i - 1, 0) % nt
    first_of_stream = (i >= 1) & (mix_tile == 0)
    last_of_stream = (i >= 1) & (mix_tile == nt - 1)

    @pl.when(i == 0)
    def _prime():
        xbc_scr[1] = jnp.zeros((S, T, CONV_DIM), F32)
        z_scr[1] = jnp.zeros((M, D_INNER), F32)
        xres_scr[1] = jnp.zeros((M, D_MODEL), F32)
        la_scr[1] = jnp.zeros((M, LANES), F32)
        dt_scr[1] = jnp.zeros((M, LANES), F32)
        laT_scr[1] = jnp.zeros((LANES, M), F32)
        dtT_scr[1] = jnp.zeros((LANES, M), F32)
        before_scr[...] = jnp.zeros((S, q, CONV_DIM), F32)
        h_scr[...] = jnp.zeros((S, SSM_GROUPS, SSM_STATE, GROUP_COLS), F32)

    @pl.when(first_of_stream)
    def _load_state():
        for s in range(S):
            before_scr[s] = jnp.zeros((q, CONV_DIM), F32)
            if zero_state:
                h_scr[s] = jnp.zeros((SSM_GROUPS, SSM_STATE, GROUP_COLS), F32)
                continue
            for m, rho in enumerate(tail_rows):
                before_scr[s, rho:rho + 1, :] = conv0_ref[s, m:m + 1, :]
            for g in range(SSM_GROUPS):
                hg = ssm0_ref[s, HEADS_PER_GROUP * g:HEADS_PER_GROUP * (g + 1)]
                h_scr[s, g] = hg.reshape(GROUP_COLS, SSM_STATE).T

    def step(w, r):
        x = x_ref[...].reshape(M, D_MODEL)
        xres_scr[w] = x
        hn = _rms(x, rmsw_ref[...])
        hn_bf = hn.astype(BF16)
        hn_p = jnp.dot(perm_ref[...], hn_bf, preferred_element_type=F32).astype(BF16)

        def proj_block(n):
            blk = jnp.dot(hn_p, win_ref[:, n * PROJ_BLOCK:(n + 1) * PROJ_BLOCK], preferred_element_type=F32)
            c0 = n * PROJ_BLOCK
            if c0 < D_INNER:
                z_scr[w, :, c0:c0 + PROJ_BLOCK] = blk
            else:
                for s in range(S):
                    xbc_scr[w, s, :, c0 - D_INNER:c0 - D_INNER + PROJ_BLOCK] = blk[s * T:(s + 1) * T]

        def dt_block():
            dt_raw = jnp.dot(hn_p, wdt_ref[...], preferred_element_type=F32)
            dt = _softplus(dt_raw + dtb_ref[...])
            dta = dt * (-jnp.exp(alog_ref[...]))
            la = _sum3(jnp.dot(cums_ref[...], jnp.concatenate(_split3(dta), axis=1), preferred_element_type=F32))
            la_scr[w] = la
            dt_scr[w] = dt
            laT_scr[w] = la.T
            dtT_scr[w] = dt.T

        tok_t = _chunk_token(lax.broadcasted_iota(jnp.int32, (q, HEADS_PER_GROUP * q), 0), q)
        tok_s = _chunk_token(lax.broadcasted_iota(jnp.int32, (q, HEADS_PER_GROUP * q), 1), q)
        causal = tok_t >= tok_s
        lane_cat = lax.broadcasted_iota(jnp.int32, (q, HEADS_PER_GROUP * q), 1)
        lane_g = lax.broadcasted_iota(jnp.int32, (q, GROUP_COLS), 1)
        head_masks = [(lane_g >= SSM_HEAD_DIM * hr) & (lane_g < SSM_HEAD_DIM * (hr + 1))
                      for hr in range(HEADS_PER_GROUP)]
        sub_iota = lax.broadcasted_iota(jnp.int32, (SUBLANES, CONV_DIM), 0)

        def mix_chunk(s, c):
            base = c * q
            g0 = s * T + c * q

            def halo(v, sh):
                cur = xbc_scr[r, s, base + SUBLANES * v:base + SUBLANES * (v + 1), :]
                if c == 0:
                    prv = before_scr[s, SUBLANES * v:SUBLANES * (v + 1), :]
                else:
                    prv = xbc_scr[r, s, base - q + SUBLANES * v:base - q + SUBLANES * (v + 1), :]
                return jnp.where(sub_iota >= sh, pltpu.roll(cur, sh, 0), pltpu.roll(prv, sh, 0))

            acc = convb_ref[...] + convw_ref[CONV_WIDTH - 1:CONV_WIDTH, :] * xbc_scr[r, s, base:base + q, :]
            for d in range(1, CONV_WIDTH):
                pieces = []
                for j in range(min(d, nv)):
                    jj = j - d
                    pieces.append(halo(jj % nv, -(jj // nv)))
                if q - SUBLANES * d > 0:
                    pieces.append(xbc_scr[r, s, base:base + q - SUBLANES * d, :])
                src = pieces[0] if len(pieces) == 1 else jnp.concatenate(pieces, axis=0)
                acc = acc + convw_ref[CONV_WIDTH - 1 - d:CONV_WIDTH - d, :] * src
            xc = _silu(acc.astype(BF16))
            xs_bf = xc[:, :D_INNER]
            xs = xs_bf.astype(F32)
            b_bf = xc[:, D_INNER:D_INNER + GN]
            c_bf = xc[:, D_INNER + GN:]

            la_c = la_scr[r, g0:g0 + q, :]
            dt_c = dt_scr[r, g0:g0 + q, :]
            laT_c = laT_scr[r, :, g0:g0 + q]
            dtT_c = dtT_scr[r, :, g0:g0 + q]
            last = la_c[q - 1:q, :]
            e_la = _expand_heads(jnp.exp(la_c))
            w_in = _expand_heads(jnp.exp(last - la_c) * dt_c)
            e_last = _expand_heads(jnp.exp(last))
            xw_bf = (xs * w_in).astype(BF16)
            if q == SSM_HEAD_DIM:
                la_cols = _expand_heads(la_c)

            ys = []
            for g in range(SSM_GROUPS):
                gs = slice(g * GROUP_COLS, (g + 1) * GROUP_COLS)
                h0 = HEADS_PER_GROUP * g
                bg = b_bf[:, g * SSM_STATE:(g + 1) * SSM_STATE]
                cg = c_bf[:, g * SSM_STATE:(g + 1) * SSM_STATE]
                cb = lax.dot_general(cg, jnp.concatenate([bg] * HEADS_PER_GROUP, axis=0), NT_DIMS,
                                     preferred_element_type=F32)
                if q == SSM_HEAD_DIM:
                    la_col = la_cols[:, gs]
                else:
                    la_col = la_c[:, h0 + HEADS_PER_GROUP - 1:h0 + HEADS_PER_GROUP]
                    for hr in range(HEADS_PER_GROUP - 2, -1, -1):
                        la_col = jnp.where(lane_cat < (hr + 1) * q, la_c[:, h0 + hr:h0 + hr + 1], la_col)
                la_row = jnp.concatenate([laT_c[h0 + hr:h0 + hr + 1, :] for hr in range(HEADS_PER_GROUP)], axis=1)
                dt_row = jnp.concatenate([dtT_c[h0 + hr:h0 + hr + 1, :] for hr in range(HEADS_PER_GROUP)], axis=1)
                dec = jnp.exp(jnp.where(causal, la_col - la_row, NEG_BIG).astype(BF16))
                m_cat = cb.astype(BF16) * dec * dt_row.astype(BF16)
                xg = xs_bf[:, gs]
                x_bd = jnp.concatenate([jnp.where(head_masks[hr], xg, jnp.zeros_like(xg))
                                        for hr in range(HEADS_PER_GROUP)], axis=0)
                y_in = jnp.dot(m_cat, x_bd, preferred_element_type=F32)
                hT = h_scr[s, g]
                y_st = jnp.dot(cg, hT.astype(BF16), preferred_element_type=F32)
                ys.append(y_in + y_st * e_la[:, gs])
                upd = lax.dot_general(bg, xw_bf[:, gs], TN_DIMS, preferred_element_type=F32)
                h_scr[s, g] = hT * e_last[:, gs] + upd

            y = jnp.concatenate(ys, axis=1) + dexp_ref[...] * xs
            gated = y * _silu(z_scr[r, g0:g0 + q, :])
            parts = []
            for k in range(D_INNER // NORM_GROUP):
                gk = gated[:, k * NORM_GROUP:(k + 1) * NORM_GROUP]
                parts.append(gk * lax.rsqrt(jnp.mean(gk * gk, axis=-1, keepdims=True) + EPS))
            y_scr[g0:g0 + q, :] = (jnp.concatenate(parts, axis=1) * normw_ref[...]).astype(BF16)

        def mix_finish():
            for s in range(S):
                before_scr[s] = xbc_scr[r, s, T - q:T, :]
            y_nat = jnp.dot(permT_ref[...], y_scr[...], preferred_element_type=F32).astype(BF16)
            out = jnp.dot(y_nat, wout_ref[...], preferred_element_type=F32) + xres_scr[r]
            out_ref[...] = out.reshape(S, T, D_MODEL)

            @pl.when(last_of_stream)
            def _emit_state():
                for s in range(S):
                    for m, rho in enumerate(tail_rows):
                        convn_ref[s, m:m + 1, :] = before_scr[s, rho:rho + 1, :]
                    for g in range(SSM_GROUPS):
                        hg = h_scr[s, g].T.reshape(HEADS_PER_GROUP, SSM_HEAD_DIM, SSM_STATE)
                        ssmn_ref[s, HEADS_PER_GROUP * g:HEADS_PER_GROUP * (g + 1)] = hg

        proj_items = [functools.partial(proj_block, n) for n in range(ZXBC // PROJ_BLOCK)] + [dt_block]
        mix_items = [functools.partial(mix_chunk, s, c) for s in range(S) for c in range(n_chunks)]
        proj_items[0]()
        rest = np.array_split(np.arange(1, len(proj_items)), len(mix_items))
        for mix, idxs in zip(mix_items, rest):
            mix()
            for k in idxs:
                proj_items[k]()
        mix_finish()

    @pl.when(i % 2 == 0)
    def _even():
        step(0, 1)

    @pl.when(i % 2 == 1)
    def _odd():
        step(1, 0)


def _const_spec(shape):
    zeros = (0,) * len(shape)
    return pl.BlockSpec(shape, lambda *_: zeros, pipeline_mode=pl.Buffered(1))


def _row_order_matrices(M, q):
    rows = np.arange(M)
    chunk = rows // q
    tok_of_row = _chunk_token(rows, q)
    same = chunk[:, None] == chunk[None, :]
    perm = same & (tok_of_row[:, None] == (rows % q)[None, :])
    cums = same & (tok_of_row[:, None] >= tok_of_row[None, :])
    as_bf = lambda m: jnp.asarray(m, dtype=BF16)
    return as_bf(perm), as_bf(perm.T), as_bf(cums)


def _ssd_layer(x, conv0, ssm0, rms_w, w_zxbc, w_dt, conv_w, conv_b, dtb, alog, d_exp, norm_w, w_out, *, S, T, q):
    nb, seq, _ = x.shape
    assert nb % S == 0 and seq % T == 0 and T % q == 0 and q % SUBLANES == 0
    assert (conv0 is None) == (ssm0 is None)
    zero_state = conv0 is None
    M = S * T
    nt = seq // T
    n_tiles = (nb // S) * nt
    kern = functools.partial(_ssd_kernel, S=S, T=T, q=q, nt=nt, zero_state=zero_state)

    def proj_tile(i):
        ia = jnp.minimum(i, n_tiles - 1)
        return ia // nt, ia % nt

    def mix_tile(i):
        ib = jnp.maximum(i - 1, 0)
        return ib // nt, ib % nt

    in_specs = [
        pl.BlockSpec((S, T, D_MODEL), lambda i: (*proj_tile(i), 0)),
        _const_spec((1, D_MODEL)),
        _const_spec((D_MODEL, ZXBC)),
        _const_spec((D_MODEL, LANES)),
        _const_spec((M, M)),
        _const_spec((M, M)),
        _const_spec((M, M)),
        _const_spec((CONV_WIDTH, CONV_DIM)),
        _const_spec((1, CONV_DIM)),
        _const_spec((1, LANES)),
        _const_spec((1, LANES)),
        _const_spec((1, D_INNER)),
        _const_spec((1, D_INNER)),
        _const_spec((D_INNER, D_MODEL)),
    ]
    if not zero_state:
        in_specs += [
            pl.BlockSpec((S, CONV_WIDTH - 1, CONV_DIM), lambda i: (mix_tile(i)[0], 0, 0)),
            pl.BlockSpec((S, SSM_HEADS, SSM_HEAD_DIM, SSM_STATE), lambda i: (mix_tile(i)[0], 0, 0, 0)),
        ]
    out_specs = [
        pl.BlockSpec((S, T, D_MODEL), lambda i: (*mix_tile(i), 0)),
        pl.BlockSpec((S, CONV_WIDTH - 1, CONV_DIM), lambda i: (mix_tile(i)[0], 0, 0)),
        pl.BlockSpec((S, SSM_HEADS, SSM_HEAD_DIM, SSM_STATE), lambda i: (mix_tile(i)[0], 0, 0, 0)),
    ]
    out_shape = [
        jax.ShapeDtypeStruct((nb, seq, D_MODEL), F32),
        jax.ShapeDtypeStruct((nb, CONV_WIDTH - 1, CONV_DIM), F32),
        jax.ShapeDtypeStruct((nb, SSM_HEADS, SSM_HEAD_DIM, SSM_STATE), F32),
    ]
    scratch = [
        pltpu.VMEM((2, S, T, CONV_DIM), F32),
        pltpu.VMEM((S, q, CONV_DIM), F32),
        pltpu.VMEM((2, M, D_INNER), F32),
        pltpu.VMEM((2, M, D_MODEL), F32),
        pltpu.VMEM((2, M, LANES), F32),
        pltpu.VMEM((2, M, LANES), F32),
        pltpu.VMEM((2, LANES, M), F32),
        pltpu.VMEM((2, LANES, M), F32),
        pltpu.VMEM((M, D_INNER), BF16),
        pltpu.VMEM((S, SSM_GROUPS, SSM_STATE, GROUP_COLS), F32),
    ]
    return pl.pallas_call(
        kern,
        grid=(n_tiles + 1,),
        in_specs=in_specs,
        out_specs=out_specs,
        out_shape=out_shape,
        scratch_shapes=scratch,
        compiler_params=pltpu.CompilerParams(
            dimension_semantics=("arbitrary",),
            vmem_limit_bytes=VMEM_LIMIT_BYTES),
        name=f"ssd_layer_T{T}",
    )(x, rms_w, w_zxbc, w_dt, *_row_order_matrices(M, q), conv_w, conv_b, dtb, alog, d_exp, norm_w, w_out,
      *(() if zero_state else (conv0, ssm0)))


def _gmlp_kernel(x_ref, rmsw_ref, win_ref, lnw_ref, lnb_ref, wsp_ref, bexp_ref, wout_ref, fw_ref,
                 *refs, S, T, q, emit_v):
    if emit_v:
        out_ref, v_ref, proj_scr, y_scr = refs
    else:
        out_ref, proj_scr, y_scr = refs
        v_ref = None
    M = S * T
    x = x_ref[...].reshape(M, D_MODEL)
    hn_bf = _rms(x, rmsw_ref[...]).astype(BF16)
    for n in range(3 * G_WIDTH // PROJ_BLOCK):
        cols = slice(n * PROJ_BLOCK, (n + 1) * PROJ_BLOCK)
        proj_scr[:, cols] = jnp.dot(hn_bf, win_ref[:, cols], preferred_element_type=F32)

    row = lax.broadcasted_iota(jnp.int32, (q, q), 0)
    col = lax.broadcasted_iota(jnp.int32, (q, q), 1)
    block_causal = (col // CAUSAL_CHUNK) <= (row // CAUSAL_CHUNK)
    wms = [jnp.where(block_causal, wsp_ref[g, 0:q, 0:q], 0.0).astype(BF16) for g in range(G_GROUPS)]

    for s in range(S):
        for c in range(T // q):
            g0 = s * T + c * q
            v = _gelu(proj_scr[g0:g0 + q, G_WIDTH:2 * G_WIDTH])
            vc = v - jnp.mean(v, axis=-1, keepdims=True)
            vn = vc * lax.rsqrt(jnp.mean(vc * vc, axis=-1, keepdims=True) + EPS) * lnw_ref[...] + lnb_ref[...]
            if emit_v:
                v_ref[s, c * q:(c + 1) * q, :] = vn
            vn_bf = vn.astype(BF16)
            for g in range(G_GROUPS):
                gc = slice(g * G_GROUP_DIM, (g + 1) * G_GROUP_DIM)
                sp = jnp.dot(wms[g], vn_bf[:, gc], preferred_element_type=F32) + bexp_ref[0:q, gc]
                u = _gelu(proj_scr[g0:g0 + q, g * G_GROUP_DIM:(g + 1) * G_GROUP_DIM])
                z = proj_scr[g0:g0 + q, 2 * G_WIDTH + g * G_GROUP_DIM:2 * G_WIDTH + (g + 1) * G_GROUP_DIM]
                y_scr[g0:g0 + q, gc] = (u * sp * _silu(z)).astype(BF16)

    out = jnp.dot(y_scr[...], wout_ref[...], preferred_element_type=F32) + x
    out_ref[...] = _rms(out, fw_ref[...]).reshape(S, T, D_MODEL)


def _gmlp_layer(x, rms_w, w_in, ln_w, ln_b, w_sp, b_exp, w_out, final_w, *, S, T, q, emit_v):
    nb, seq, _ = x.shape
    assert nb % S == 0 and seq % T == 0 and T % q == 0
    M = S * T
    kern = functools.partial(_gmlp_kernel, S=S, T=T, q=q, emit_v=emit_v)
    in_specs = [
        pl.BlockSpec((S, T, D_MODEL), lambda b, t: (b, t, 0)),
        _const_spec((1, D_MODEL)),
        _const_spec((D_MODEL, 3 * G_WIDTH)),
        _const_spec((1, G_WIDTH)),
        _const_spec((1, G_WIDTH)),
        _const_spec((G_GROUPS, G_CHUNK, G_CHUNK)),
        _const_spec((G_CHUNK, G_WIDTH)),
        _const_spec((G_WIDTH, D_MODEL)),
        _const_spec((1, D_MODEL)),
    ]
    out_specs = [pl.BlockSpec((S, T, D_MODEL), lambda b, t: (b, t, 0))]
    out_shape = [jax.ShapeDtypeStruct((nb, seq, D_MODEL), F32)]
    if emit_v:
        out_specs.append(pl.BlockSpec((S, T, G_WIDTH), lambda b, t: (b, t, 0)))
        out_shape.append(jax.ShapeDtypeStruct((nb, seq, G_WIDTH), F32))
    scratch = [pltpu.VMEM((M, 3 * G_WIDTH), F32), pltpu.VMEM((M, G_WIDTH), BF16)]
    return pl.pallas_call(
        kern,
        grid=(nb // S, seq // T),
        in_specs=in_specs,
        out_specs=out_specs,
        out_shape=out_shape,
        scratch_shapes=scratch,
        compiler_params=pltpu.CompilerParams(
            dimension_semantics=("arbitrary", "arbitrary"),
            vmem_limit_bytes=VMEM_LIMIT_BYTES),
        name=f"gmlp_layer_T{T}",
    )(x, rms_w, w_in, ln_w, ln_b, w_sp, b_exp, w_out, final_w)


def _tiling(seq, chunk, max_tile):
    return min(seq, max_tile), min(chunk, seq)


def kernel(x_prompt, x_sample, cache_conv, state_ssm, norm_w, final_norm_w, a_w_in, a_conv_w, a_conv_b,
           a_dt_bias, a_log, a_d, a_norm_w, a_w_out, b_w_in, b_ln_w, b_ln_b, b_w_sp, b_b_sp, b_w_out):
    assert a_w_in.shape[0] == 1 and b_w_in.shape[0] == 1 and norm_w.shape[0] == 2

    w_in_bf = a_w_in.astype(BF16).reshape(D_MODEL, ZXBC + SSM_HEADS)
    pad_heads = lambda v: jnp.pad(v.reshape(1, SSM_HEADS), ((0, 0), (0, LANES - SSM_HEADS)))
    ssd_params = dict(
        rms_w=norm_w[0].reshape(1, D_MODEL),
        w_zxbc=w_in_bf,
        w_dt=jnp.pad(w_in_bf[:, ZXBC:], ((0, 0), (0, LANES - SSM_HEADS))),
        conv_w=a_conv_w.reshape(CONV_WIDTH, CONV_DIM), conv_b=a_conv_b.reshape(1, CONV_DIM),
        dtb=pad_heads(a_dt_bias), alog=pad_heads(a_log),
        d_exp=jnp.repeat(a_d.reshape(SSM_HEADS), SSM_HEAD_DIM).reshape(1, D_INNER),
        norm_w=a_norm_w.reshape(1, D_INNER), w_out=a_w_out.reshape(D_INNER, D_MODEL).astype(BF16))
    gmlp_params = dict(
        rms_w=norm_w[1].reshape(1, D_MODEL), w_in=b_w_in.reshape(D_MODEL, 3 * G_WIDTH).astype(BF16),
        ln_w=b_ln_w.reshape(1, G_WIDTH), ln_b=b_ln_b.reshape(1, G_WIDTH),
        w_sp=b_w_sp.reshape(G_GROUPS, G_CHUNK, G_CHUNK),
        b_exp=jnp.repeat(b_b_sp.reshape(G_GROUPS, G_CHUNK).T, G_GROUP_DIM, axis=1),
        w_out=b_w_out.reshape(G_WIDTH, D_MODEL).astype(BF16), final_w=final_norm_w.reshape(1, D_MODEL))

    bp, lp, _ = x_prompt.shape
    bs, ls, _ = x_sample.shape

    Tp, qp = _tiling(lp, CAUSAL_CHUNK, SSD_TILE)
    hp, conv_p, ssm_p = _ssd_layer(x_prompt, None, None, **ssd_params, S=1, T=Tp, q=qp)
    Tg, qg = _tiling(lp, G_CHUNK, GMLP_TILE)
    (y_prompt,) = _gmlp_layer(hp, **gmlp_params, S=1, T=Tg, q=qg, emit_v=False)

    Ts, qs = _tiling(ls, CAUSAL_CHUNK, SSD_TILE)
    hs, conv_s, ssm_s = _ssd_layer(x_sample, cache_conv.reshape(bs, CONV_WIDTH - 1, CONV_DIM),
                                   state_ssm.reshape(bs, SSM_HEADS, SSM_HEAD_DIM, SSM_STATE),
                                   **ssd_params, S=math.gcd(bs, SSD_SAMPLE_STREAMS), T=Ts, q=qs)
    Tg, qg = _tiling(ls, G_CHUNK, GMLP_TILE)
    y_sample, v_s = _gmlp_layer(hs, **gmlp_params, S=math.gcd(bs, GMLP_SAMPLE_STREAMS), T=Tg, q=qg, emit_v=True)

    lead = lambda a: a.reshape((1,) + a.shape)
    return (y_prompt, y_sample, lead(conv_p), lead(ssm_p), lead(conv_s), lead(ssm_s), lead(v_s))
```
